```python
import jax, jax.numpy as jnp
from jax import lax
import numpy as np

D_MODEL = 1024
BATCH = 2
SEQ = 8192
DEPTH = 4

PLE_DIM = 256
D_FF = 2816
CONV_DIM = D_MODEL
CONV_K = 31
N_HEADS = 16
N_KV = 4
HPG = N_HEADS // N_KV
HEAD_DIM = 64
ROPE_DIM = HEAD_DIM // 4
ROPE_THETA = 500000.0
CMP_LEN = 32
CMP_STRIDE = 16
CMP_HID = 256
SEL_LEN = 64
N_SEL = 16
WINDOW = 512
Q_BLOCK = 64
NEG = -1e30
EPS = 1e-6

KV_W = N_KV * HEAD_DIM
OFF_Q = 2 * CONV_DIM
OFF_KV = OFF_Q + N_HEADS * HEAD_DIM
OFF_GATE = OFF_KV + 6 * KV_W
OFF_MERGE = OFF_GATE + 3 * N_HEADS
N_IN = OFF_MERGE + 2 * D_MODEL

kernel_name = "hybrid_conformer_nsa_macaron_block"


def rms_norm(x, g):
    xf = x.astype(jnp.float32)
    y = xf * lax.rsqrt(jnp.mean(xf * xf, axis=-1, keepdims=True) + EPS)
    return (y * g.astype(jnp.float32)).astype(x.dtype)


def layer_norm(x, g, b):
    xf = x.astype(jnp.float32)
    mu = jnp.mean(xf, axis=-1, keepdims=True)
    var = jnp.mean(jnp.square(xf - mu), axis=-1, keepdims=True)
    y = (xf - mu) * lax.rsqrt(var + EPS)
    return (y * g.astype(jnp.float32) + b.astype(jnp.float32)).astype(x.dtype)


def swiglu_ffn(x, norm_g, w_in, w_out):
    a, b = jnp.split(rms_norm(x, norm_g) @ w_in, 2, axis=-1)
    return (jax.nn.silu(a) * b) @ w_out


def partial_rope(x, pos):
    half = ROPE_DIM // 2
    inv = ROPE_THETA ** (-jnp.arange(half, dtype=jnp.float32) / half)
    ang = pos.astype(jnp.float32)[:, None] * inv[None, :]
    cos = jnp.cos(ang)[:, None, :]
    sin = jnp.sin(ang)[:, None, :]
    xr = x[..., :ROPE_DIM].astype(jnp.float32)
    x1, x2 = xr[..., :half], xr[..., half:]
    rot = jnp.concatenate([x1 * cos - x2 * sin, x2 * cos + x1 * sin], axis=-1).astype(x.dtype)
    return jnp.concatenate([rot, x[..., ROPE_DIM:]], axis=-1)


def masked_softmax(s, mask):
    p = jax.nn.softmax(jnp.where(mask, s, NEG), axis=-1)
    return jnp.where(mask, p, 0.0)


def conv_module(u, conv_w, conv_b, ln_g, ln_b, w_out):
    a, gate = jnp.split(u, 2, axis=-1)
    h = a * jax.nn.sigmoid(gate)
    h = lax.conv_general_dilated(
        h, conv_w[:, None, :].astype(h.dtype), window_strides=(1,),
        padding=[(CONV_K - 1, 0)], dimension_numbers=("NWC", "WIO", "NWC"),
        feature_group_count=CONV_DIM) + conv_b
    h = jax.nn.silu(layer_norm(h, ln_g, ln_b))
    return h @ w_out


def compress(kv, pos_emb, w1, b1, w2):
    b_, t_ = kv.shape[:2]
    sub = kv.reshape(b_, t_ // CMP_STRIDE, CMP_STRIDE, N_KV, HEAD_DIM)
    blk = jnp.concatenate([sub[:, :-1], sub[:, 1:]], axis=2)
    blk = blk + pos_emb[None, None, :, None, :]
    blk = jnp.moveaxis(blk, 3, 2).reshape(b_, -1, N_KV, CMP_LEN * HEAD_DIM)
    return jax.nn.gelu(blk @ w1 + b1) @ w2


def nsa(q, kc, vc, ks, vs, kw, vw, gates):
    b_, t_ = q.shape[:2]
    n_cmp = kc.shape[1]
    n_blk = t_ // SEL_LEN
    n_sel = min(N_SEL, n_blk)
    scale = HEAD_DIM ** -0.5
    cmp_end = jnp.arange(n_cmp) * CMP_STRIDE + CMP_LEN - 1
    ci = jnp.arange(n_cmp)[:, None]
    sj = jnp.arange(n_blk)[None, :]
    cmp_to_sel = ((ci * CMP_STRIDE < (sj + 1) * SEL_LEN)
                  & (ci * CMP_STRIDE + CMP_LEN > sj * SEL_LEN)).astype(jnp.float32)
    blk_start = jnp.arange(n_blk) * SEL_LEN

    def to_blocks(a):
        a = a.reshape(b_, n_blk, SEL_LEN, N_KV, HEAD_DIM).transpose(0, 3, 1, 2, 4)
        return a.reshape(b_ * N_KV, n_blk, SEL_LEN, HEAD_DIM)

    ks_blk, vs_blk = to_blocks(ks), to_blocks(vs)
    pad = ((0, 0), (WINDOW, 0), (0, 0), (0, 0))
    kw_pad, vw_pad = jnp.pad(kw, pad), jnp.pad(vw, pad)
    gather = jax.vmap(lambda a, i: jnp.take(a, i, axis=0))

    def block(c):
        start = c * Q_BLOCK
        t = start + jnp.arange(Q_BLOCK)
        qb = lax.dynamic_slice_in_dim(q, start, Q_BLOCK, axis=1)
        gb = lax.dynamic_slice_in_dim(gates, start, Q_BLOCK, axis=1)
        s = jnp.einsum("bqghd,bngd->bghqn", qb, kc, preferred_element_type=jnp.float32) * scale
        p_c = masked_softmax(s, cmp_end[None, :] <= t[:, None])
        o_c = jnp.einsum("bghqn,bngd->bqghd", p_c.astype(vc.dtype), vc)
        imp = jnp.einsum("bghqn,ns->bgqs", p_c, cmp_to_sel)
        cur = (t // SEL_LEN)[:, None]
        jj = jnp.arange(n_blk)[None, :]
        forced = (jj == 0) | (jj == cur) | (jj == cur - 1)
        valid = blk_start[None, :] <= t[:, None]
        imp = jnp.where(valid, jnp.where(forced, 1e4, imp), -1.0)
        top_val, top_idx = lax.top_k(imp, n_sel)
        idx_flat = top_idx.reshape(b_ * N_KV, Q_BLOCK * n_sel)
        k_sel = gather(ks_blk, idx_flat).reshape(b_, N_KV, Q_BLOCK, n_sel * SEL_LEN, HEAD_DIM)
        v_sel = gather(vs_blk, idx_flat).reshape(b_, N_KV, Q_BLOCK, n_sel * SEL_LEN, HEAD_DIM)
        s = jnp.einsum("bqghd,bgqmd->bghqm", qb, k_sel, preferred_element_type=jnp.float32) * scale
        kpos = (top_idx[..., None] * SEL_LEN + jnp.arange(SEL_LEN)).reshape(b_, N_KV, Q_BLOCK, n_sel * SEL_LEN)
        m = (kpos <= t[:, None]) & jnp.repeat(top_val >= 0.0, SEL_LEN, axis=-1)
        p_s = masked_softmax(s, m[:, :, None])
        o_s = jnp.einsum("bghqm,bgqmd->bqghd", p_s.astype(v_sel.dtype), v_sel)
        kwb = lax.dynamic_slice_in_dim(kw_pad, start, WINDOW + Q_BLOCK, axis=1)
        vwb = lax.dynamic_slice_in_dim(vw_pad, start, WINDOW + Q_BLOCK, axis=1)
        s = jnp.einsum("bqghd,bsgd->bghqs", qb, kwb, preferred_element_type=jnp.float32) * scale
        spos = start - WINDOW + jnp.arange(WINDOW + Q_BLOCK)
        m = (spos[None, :] >= 0) & (spos[None, :] <= t[:, None]) & (spos[None, :] > t[:, None] - WINDOW)
        p_w = masked_softmax(s, m)
        o_w = jnp.einsum("bghqs,bsgd->bqghd", p_w.astype(vwb.dtype), vwb)
        return gb[..., 0:1] * o_c + gb[..., 1:2] * o_s + gb[..., 2:3] * o_w

    out = lax.map(block, jnp.arange(t_ // Q_BLOCK))
    return jnp.moveaxis(out, 0, 1).reshape(b_, t_, N_HEADS * HEAD_DIM)


def setup_inputs(seed: int = 0) -> dict:
    key = jax.random.key(seed)
    ks = iter(jax.random.split(key, 40))

    def nrm(shape, scale):
        return jax.random.normal(next(ks), shape, jnp.float32) * scale

    def gain(shape):
        return 1.0 + nrm(shape, 0.02)

    L, D = DEPTH, D_MODEL
    return {
        "x": nrm((BATCH, SEQ, D), 1.0),
        "p": nrm((DEPTH, BATCH, SEQ, PLE_DIM), 1.0),
        "ffn1_norm": gain((L, D)),
        "ffn1_w_in": nrm((L, D, 2 * D_FF), D ** -0.5),
        "ffn1_w_out": nrm((L, D_FF, D), D_FF ** -0.5),
        "mix_norm": gain((L, D)),
        "w_in": nrm((L, D, N_IN), D ** -0.5),
        "b_in": nrm((L, N_IN), 0.02),
        "conv_w": nrm((L, CONV_K, CONV_DIM), CONV_K ** -0.5),
        "conv_b": nrm((L, CONV_DIM), 0.02),
        "conv_ln_g": gain((L, CONV_DIM)),
        "conv_ln_b": nrm((L, CONV_DIM), 0.02),
        "conv_w_out": nrm((L, CONV_DIM, D), CONV_DIM ** -0.5),
        "q_norm": gain((L, HEAD_DIM)),
        "k_norm": gain((L, 3, HEAD_DIM)),
        "cmp_pos": nrm((L, 2, CMP_LEN, HEAD_DIM), 0.02),
        "cmp_w1": nrm((L, 2, CMP_LEN * HEAD_DIM, CMP_HID), (CMP_LEN * HEAD_DIM) ** -0.5),
        "cmp_b1": nrm((L, 2, CMP_HID), 0.02),
        "cmp_w2": nrm((L, 2, CMP_HID, HEAD_DIM), CMP_HID ** -0.5),
        "nsa_w_out": nrm((L, N_HEADS * HEAD_DIM, D), (N_HEADS * HEAD_DIM) ** -0.5),
        "w_out": nrm((L, D, D), D ** -0.5),
        "ffn2_norm": gain((L, D)),
        "ffn2_w_in": nrm((L, D, 2 * D_FF), D ** -0.5),
        "ffn2_w_out": nrm((L, D_FF, D), D_FF ** -0.5),
        "ple_norm": gain((L, D)),
        "ple_w_gate": nrm((L, D, D), D ** -0.5),
        "ple_w_proj": nrm((L, PLE_DIM, D), PLE_DIM ** -0.5),
        "ple_post_norm": gain((L, D)),
    }


def reference(x, p, ffn1_norm, ffn1_w_in, ffn1_w_out, mix_norm, w_in, b_in, conv_w, conv_b,
              conv_ln_g, conv_ln_b, conv_w_out, q_norm, k_norm, cmp_pos, cmp_w1, cmp_b1, cmp_w2,
              nsa_w_out, w_out, ffn2_norm, ffn2_w_in, ffn2_w_out, ple_norm, ple_w_gate,
              ple_w_proj, ple_post_norm):
    b_, t_, _ = x.shape
    pos = jnp.arange(t_)
    n_cmp = t_ // CMP_STRIDE - 1
    cmp_pos_ids = jnp.arange(n_cmp) * CMP_STRIDE + CMP_LEN - 1
    for i in range(DEPTH):
        x = x + 0.5 * swiglu_ffn(x, ffn1_norm[i], ffn1_w_in[i], ffn1_w_out[i])
        h = rms_norm(x, mix_norm[i])
        u = h @ w_in[i] + b_in[i]
        y_conv = conv_module(u[..., :OFF_Q], conv_w[i], conv_b[i], conv_ln_g[i], conv_ln_b[i], conv_w_out[i])
        q = u[..., OFF_Q:OFF_KV].reshape(b_, t_, N_HEADS, HEAD_DIM)
        q = partial_rope(rms_norm(q, q_norm[i]), pos).reshape(b_, t_, N_KV, HPG, HEAD_DIM)
        k_ct, v_ct, k_s, v_s, k_w, v_w = [
            u[..., OFF_KV + j * KV_W: OFF_KV + (j + 1) * KV_W].reshape(b_, t_, N_KV, HEAD_DIM)
            for j in range(6)]
        k_c = compress(k_ct, cmp_pos[i, 0], cmp_w1[i, 0], cmp_b1[i, 0], cmp_w2[i, 0])
        v_c = compress(v_ct, cmp_pos[i, 1], cmp_w1[i, 1], cmp_b1[i, 1], cmp_w2[i, 1])
        k_c = partial_rope(rms_norm(k_c, k_norm[i, 0]), cmp_pos_ids)
        k_s = partial_rope(rms_norm(k_s, k_norm[i, 1]), pos)
        k_w = partial_rope(rms_norm(k_w, k_norm[i, 2]), pos)
        nsa_gates = jax.nn.sigmoid(u[..., OFF_GATE:OFF_MERGE]).reshape(b_, t_, N_KV, HPG, 3)
        y_nsa = nsa(q, k_c, v_c, k_s, v_s, k_w, v_w, nsa_gates) @ nsa_w_out[i]
        g_conv, g_nsa = jnp.split(jax.nn.sigmoid(u[..., OFF_MERGE:]), 2, axis=-1)
        x = x + (g_conv * y_conv + g_nsa * y_nsa) @ w_out[i]
        x = x + 0.5 * swiglu_ffn(x, ffn2_norm[i], ffn2_w_in[i], ffn2_w_out[i])
        e = rms_norm(p[i] @ ple_w_proj[i], ple_post_norm[i])
        x = x + jax.nn.sigmoid(rms_norm(x, ple_norm[i]) @ ple_w_gate[i]) * e
    return x
```

```python
import functools

import jax
import jax.numpy as jnp
from jax import lax
from jax.experimental import pallas as pl
from jax.experimental.pallas import tpu as pltpu

D_MODEL = 1024
PLE_DIM = 256
D_FF = 2816
CONV_DIM = D_MODEL
CONV_K = 31
N_HEADS = 16
N_KV = 4
HPG = N_HEADS // N_KV
HEAD_DIM = 64
ROPE_DIM = HEAD_DIM // 4
ROPE_THETA = 500000.0
CMP_LEN = 32
CMP_STRIDE = 16
CMP_HID = 256
SEL_LEN = 64
N_SEL = 16
WINDOW = 512
NEG = -1e30
EPS = 1e-6
KV_W = N_KV * HEAD_DIM
OFF_Q = 2 * CONV_DIM
OFF_KV = OFF_Q + N_HEADS * HEAD_DIM
OFF_GATE = OFF_KV + 6 * KV_W
OFF_MERGE = OFF_GATE + 3 * N_HEADS
N_IN = OFF_MERGE + 2 * D_MODEL

MXU_DTYPE = jnp.bfloat16
F32 = jnp.float32
LANES = 128
VMEM_LIMIT = 56 * 1024 * 1024

TM = 1024
TN = 256
TF = 256
TM_OUT = 512
TT = 256
HALO = 32
CONV_ROWS = 32
CONV_COLS = 256
TQ = 128
TK = 256
MASK_BIG = 2.0 ** 100


def _params(*sem):
    return pltpu.CompilerParams(dimension_semantics=sem, vmem_limit_bytes=VMEM_LIMIT)


def _rms(x, g):
    return x * lax.rsqrt(jnp.mean(x * x, axis=-1, keepdims=True) + EPS) * g


def _dot(a, b):
    return jnp.dot(a, b, preferred_element_type=F32)


def _ffn_kernel(x_ref, g_ref, wa_ref, wb_ref, wo_ref, o_ref, h_scr, acc_scr):
    f = pl.program_id(1)

    @pl.when(f == 0)
    def _():
        h_scr[...] = _rms(x_ref[...], g_ref[...]).astype(h_scr.dtype)
        acc_scr[...] = jnp.zeros_like(acc_scr)

    h = h_scr[...]
    a = _dot(h, wa_ref[...])
    b = _dot(h, wb_ref[...])
    act = (a * jax.nn.sigmoid(a) * b).astype(wo_ref.dtype)
    acc_scr[...] += _dot(act, wo_ref[...])

    @pl.when(f == pl.num_programs(1) - 1)
    def _():
        o_ref[...] = x_ref[...] + 0.5 * acc_scr[...]


def _ffn(x, g, w_in, w_out):
    n = x.shape[0]
    nf = D_FF // TF
    return pl.pallas_call(
        _ffn_kernel,
        grid=(n // TM, nf),
        in_specs=[
            pl.BlockSpec((TM, D_MODEL), lambda i, f: (i, 0)),
            pl.BlockSpec((1, D_MODEL), lambda i, f: (0, 0)),
            pl.BlockSpec((D_MODEL, TF), lambda i, f: (0, f)),
            pl.BlockSpec((D_MODEL, TF), lambda i, f: (0, f + nf)),
            pl.BlockSpec((TF, D_MODEL), lambda i, f: (f, 0)),
        ],
        out_specs=pl.BlockSpec((TM, D_MODEL), lambda i, f: (i, 0)),
        out_shape=jax.ShapeDtypeStruct((n, D_MODEL), F32),
        scratch_shapes=[pltpu.VMEM((TM, D_MODEL), w_in.dtype), pltpu.VMEM((TM, D_MODEL), F32)],
        compiler_params=_params("parallel", "arbitrary"),
        name="ffn",
    )(x, g, w_in, w_in, w_out)


def _norm_to_scratch(x_ref, g_ref, h_scr):
    @pl.when(pl.program_id(1) == 0)
    def _():
        h_scr[...] = _rms(x_ref[...], g_ref[...]).astype(h_scr.dtype)


def _glu_kernel(x_ref, g_ref, wa_ref, wg_ref, ba_ref, bg_ref, o_ref, h_scr):
    _norm_to_scratch(x_ref, g_ref, h_scr)
    h = h_scr[...]
    a = _dot(h, wa_ref[...]) + ba_ref[...]
    gate = _dot(h, wg_ref[...]) + bg_ref[...]
    o_ref[...] = a * jax.nn.sigmoid(gate)


def _act_proj_kernel(x_ref, g_ref, w_ref, b_ref, o_ref, h_scr, *, sigmoid):
    _norm_to_scratch(x_ref, g_ref, h_scr)
    y = _dot(h_scr[...], w_ref[...]) + b_ref[...]
    o_ref[...] = (jax.nn.sigmoid(y) if sigmoid else y).astype(o_ref.dtype)


def _head_norm_rope(y, ones_bd, gain, cos, sin_lo, sin_hi):
    ss = _dot((y * y).astype(ones_bd.dtype), ones_bd)
    yn = y * lax.rsqrt(ss * (1.0 / HEAD_DIM) + EPS) * gain
    width = y.shape[-1]
    half = ROPE_DIM // 2
    return (yn * cos + pltpu.roll(yn, width - half, 1) * sin_lo + pltpu.roll(yn, half, 1) * sin_hi)


def _qk_proj_kernel(x_ref, g_ref, w_ref, b_ref, hg_ref, bd_ref, cos_ref, slo_ref, shi_ref, o_ref, h_scr):
    _norm_to_scratch(x_ref, g_ref, h_scr)
    y = _dot(h_scr[...], w_ref[...]) + b_ref[...]
    out = _head_norm_rope(y, bd_ref[...], hg_ref[...], cos_ref[...], slo_ref[...], shi_ref[...])
    o_ref[...] = out.astype(o_ref.dtype)


def _row_col_specs(n_w):
    x_spec = pl.BlockSpec((TM, D_MODEL), lambda i, j: (i, 0))
    g_spec = pl.BlockSpec((1, D_MODEL), lambda i, j: (0, 0))
    w_spec = pl.BlockSpec((D_MODEL, TN), lambda i, j: (0, j))
    b_spec = pl.BlockSpec((1, TN), lambda i, j: (0, j))
    o_spec = pl.BlockSpec((TM, TN), lambda i, j: (i, j))
    return x_spec, g_spec, w_spec, b_spec, o_spec


def _glu_proj(x, g, w, b):
    n = x.shape[0]
    nj = CONV_DIM // TN
    x_spec, g_spec, w_spec, b_spec, o_spec = _row_col_specs(w.shape[1])
    w2_spec = pl.BlockSpec((D_MODEL, TN), lambda i, j: (0, j + nj))
    b2_spec = pl.BlockSpec((1, TN), lambda i, j: (0, j + nj))
    return pl.pallas_call(
        _glu_kernel,
        grid=(n // TM, nj),
        in_specs=[x_spec, g_spec, w_spec, w2_spec, b_spec, b2_spec],
        out_specs=o_spec,
        out_shape=jax.ShapeDtypeStruct((n, CONV_DIM), F32),
        scratch_shapes=[pltpu.VMEM((TM, D_MODEL), w.dtype)],
        compiler_params=_params("parallel", "arbitrary"),
        name="glu_proj",
    )(x, g, w, w, b, b)


def _act_proj(x, g, w, b, *, sigmoid, name):
    n = x.shape[0]
    x_spec, g_spec, w_spec, b_spec, o_spec = _row_col_specs(w.shape[1])
    return pl.pallas_call(
        functools.partial(_act_proj_kernel, sigmoid=sigmoid),
        grid=(n // TM, w.shape[1] // TN),
        in_specs=[x_spec, g_spec, w_spec, b_spec],
        out_specs=o_spec,
        out_shape=jax.ShapeDtypeStruct((n, w.shape[1]), F32),
        scratch_shapes=[pltpu.VMEM((TM, D_MODEL), w.dtype)],
        compiler_params=_params("parallel", "arbitrary"),
        name=name,
    )(x, g, w, b)


def _qk_proj(x, g, w, b, head_gain, ones_bd, cos, sin_lo, sin_hi, seq):
    n = x.shape[0]
    x_spec, g_spec, w_spec, b_spec, o_spec = _row_col_specs(w.shape[1])
    t_tiles = seq // TM
    tab_spec = pl.BlockSpec((TM, TN), lambda i, j: (i % t_tiles, 0))
    return pl.pallas_call(
        _qk_proj_kernel,
        grid=(n // TM, w.shape[1] // TN),
        in_specs=[x_spec, g_spec, w_spec, b_spec, b_spec,
                  pl.BlockSpec((TN, TN), lambda i, j: (0, 0)), tab_spec, tab_spec, tab_spec],
        out_specs=o_spec,
        out_shape=jax.ShapeDtypeStruct((n, w.shape[1]), w.dtype),
        scratch_shapes=[pltpu.VMEM((TM, D_MODEL), w.dtype)],
        compiler_params=_params("parallel", "arbitrary"),
        name="qk_proj",
    )(x, g, w, b, head_gain, ones_bd, cos, sin_lo, sin_hi)


def _conv_kernel(cur_ref, halo_ref, w_ref, b_ref, lg_ref, lb_ref, o_ref, sh_scr, conv_scr):
    t = pl.program_id(1)
    ext = TT + HALO - 8
    sh_scr[0, 0:HALO, :] = jnp.where(t == 0, 0.0, halo_ref[...])
    sh_scr[0, HALO:HALO + TT, :] = cur_ref[...]
    for r in range(1, 8):
        sh_scr[r, 0:ext, :] = sh_scr[0, r:r + ext, :]

    lead = HALO - (CONV_K - 1)

    def row_chunk(ri, carry):
        r0 = pl.multiple_of(ri * CONV_ROWS, CONV_ROWS)
        for c0 in range(0, CONV_DIM, CONV_COLS):
            acc = jnp.broadcast_to(b_ref[:, c0:c0 + CONV_COLS], (CONV_ROWS, CONV_COLS))
            for k in range(CONV_K):
                off = k + lead
                base = pl.multiple_of(r0 + (off // 8) * 8, 8)
                acc = acc + w_ref[k:k + 1, c0:c0 + CONV_COLS] * sh_scr[off % 8, pl.ds(base, CONV_ROWS),
                                                                       c0:c0 + CONV_COLS]
            conv_scr[pl.ds(r0, CONV_ROWS), c0:c0 + CONV_COLS] = acc
        return carry

    lax.fori_loop(0, TT // CONV_ROWS, row_chunk, 0)

    h = conv_scr[...]
    mu = jnp.mean(h, axis=-1, keepdims=True)
    var = jnp.mean(jnp.square(h - mu), axis=-1, keepdims=True)
    y = (h - mu) * lax.rsqrt(var + EPS) * lg_ref[...] + lb_ref[...]
    o_ref[...] = (y * jax.nn.sigmoid(y)).astype(o_ref.dtype)


def _conv_module(h, w, b, ln_g, ln_b, out_dtype):
    bsz, seq, _ = h.shape
    halo_per_tile = TT // HALO
    vec = pl.BlockSpec((1, CONV_DIM), lambda bi, t: (0, 0))
    return pl.pallas_call(
        _conv_kernel,
        grid=(bsz, seq // TT),
        in_specs=[
            pl.BlockSpec((None, TT, CONV_DIM), lambda bi, t: (bi, t, 0)),
            pl.BlockSpec((None, HALO, CONV_DIM), lambda bi, t: (bi, jnp.maximum(t * halo_per_tile - 1, 0), 0)),
            pl.BlockSpec((CONV_K, CONV_DIM), lambda bi, t: (0, 0)),
            vec, vec, vec,
        ],
        out_specs=pl.BlockSpec((None, TT, CONV_DIM), lambda bi, t: (bi, t, 0)),
        out_shape=jax.ShapeDtypeStruct((bsz, seq, CONV_DIM), out_dtype),
        scratch_shapes=[pltpu.VMEM((8, TT + HALO, CONV_DIM), F32), pltpu.VMEM((TT, CONV_DIM), F32)],
        compiler_params=_params("parallel", "parallel"),
        name="conv_module",
    )(h, h, w, b, ln_g, ln_b)


def _compress_kernel(a_ref, plo_ref, phi_ref, w1a_ref, w1b_ref, b1_ref, w2_ref,
                     hg_ref, cos_ref, slo_ref, shi_ref, o_ref, *, is_key):
    a = a_ref[...]
    n = a.shape[0]
    p = _dot((a + plo_ref[...]).astype(w1a_ref.dtype), w1a_ref[...])
    q = _dot((a + phi_ref[...]).astype(w1b_ref.dtype), w1b_ref[...])
    hid = p + pltpu.roll(q, n - 1, 0) + b1_ref[...]
    hid = jax.nn.gelu(hid, approximate=True)
    c = _dot(hid.astype(w2_ref.dtype), w2_ref[...])
    if is_key:
        ss = jnp.sum(c * c, axis=-1, keepdims=True)
        cn = c * lax.rsqrt(ss * (1.0 / HEAD_DIM) + EPS) * hg_ref[...]
        half = ROPE_DIM // 2
        c = cn * cos_ref[...] + pltpu.roll(cn, LANES - half, 1) * slo_ref[...] + pltpu.roll(cn, half, 1) * shi_ref[...]
    o_ref[...] = c[:, :HEAD_DIM].astype(o_ref.dtype)


def _compress(a, pos_lo, pos_hi, w1a, w1b, b1, w2, head_gain, cos, sin_lo, sin_hi, *, is_key, out_dtype):
    bsz, ng, n_sub, width = a.shape
    full = lambda *shape: pl.BlockSpec(shape, lambda bi, gi: (0,) * len(shape))
    return pl.pallas_call(
        functools.partial(_compress_kernel, is_key=is_key),
        grid=(bsz, ng),
        in_specs=[
            pl.BlockSpec((None, None, n_sub, width), lambda bi, gi: (bi, gi, 0, 0)),
            full(1, width), full(1, width), full(width, CMP_HID), full(width, CMP_HID),
            full(1, CMP_HID), full(CMP_HID, LANES), full(1, LANES),
            full(n_sub, LANES), full(n_sub, LANES), full(n_sub, LANES),
        ],
        out_specs=pl.BlockSpec((None, None, n_sub, HEAD_DIM), lambda bi, gi: (bi, gi, 0, 0)),
        out_shape=jax.ShapeDtypeStruct((bsz, ng, n_sub, HEAD_DIM), out_dtype),
        compiler_params=_params("parallel", "parallel"),
        name="compress_k" if is_key else "compress_v",
    )(a, pos_lo, pos_hi, w1a, w1b, b1, w2, head_gain, cos, sin_lo, sin_hi)


def _nsa_kernel(q_ref, kts_ref, vs_ref, ktw_ref, vw_ref, kct_ref, vc_ref, c2s_ref, g_ref, o_ref, qaug_scr):
    m_rows = HPG * TQ
    t0 = pl.program_id(2) * TQ
    q = q_ref[...].reshape(m_rows, HEAD_DIM)
    row_t = t0 + (lax.broadcasted_iota(jnp.int32, (m_rows, 1), 0) & (TQ - 1))

    n_cmp = kct_ref.shape[-1]
    s = _dot(q, kct_ref[...])
    cmp_end = lax.broadcasted_iota(jnp.int32, (1, n_cmp), 1) * CMP_STRIDE + (CMP_LEN - 1)
    cmask = cmp_end <= row_t
    s = jnp.where(cmask, s, NEG)
    p = jnp.where(cmask, jnp.exp(s - jnp.max(s, axis=-1, keepdims=True)), 0.0)
    l = jnp.sum(p, axis=-1, keepdims=True)
    pc = p / jnp.where(l > 0.0, l, 1.0)
    o_c = _dot(pc.astype(vc_ref.dtype), vc_ref[...])

    pc_sum = pc[0:TQ] + pc[TQ:2 * TQ] + pc[2 * TQ:3 * TQ] + pc[3 * TQ:4 * TQ]
    c2s = c2s_ref[...]
    hi = pc_sum.astype(c2s.dtype)
    lo = (pc_sum - hi.astype(F32)).astype(c2s.dtype)
    nt = (((1,), (1,)), ((), ()))
    imp = (lax.dot_general(c2s, hi, nt, preferred_element_type=F32)
           + lax.dot_general(c2s, lo, nt, preferred_element_type=F32))
    n_blk = imp.shape[0]
    j_io = lax.broadcasted_iota(jnp.int32, (n_blk, TQ), 0)
    t_io = t0 + lax.broadcasted_iota(jnp.int32, (n_blk, TQ), 1)
    cur = t_io // SEL_LEN
    forced = (j_io == 0) | (j_io == cur) | (j_io == cur - 1)
    valid = j_io * SEL_LEN <= t_io
    imp = jnp.where(valid, jnp.where(forced, 1e4, imp), -1.0)

    sel = jnp.zeros((n_blk, TQ), F32)
    j_f = j_io.astype(F32)
    for _ in range(min(N_SEL, n_blk)):
        mx = jnp.max(imp, axis=0, keepdims=True)
        first = jnp.min(jnp.where(imp == mx, j_f, float(n_blk)), axis=0, keepdims=True)
        pick = j_f == first
        sel = jnp.where(pick & (mx >= 0.0), 1.0, sel)
        imp = jnp.where(pick, -3e38, imp)
    bias = ((sel.T - 1.0) * MASK_BIG).astype(qaug_scr.dtype)

    qaug_scr[:, 0:HEAD_DIM] = q
    qaug_scr[:, HEAD_DIM:LANES] = jnp.zeros((m_rows, LANES - HEAD_DIM), qaug_scr.dtype)
    if n_blk < LANES:
        qaug_scr[:, LANES + n_blk:] = jnp.zeros((m_rows, LANES - n_blk), qaug_scr.dtype)
    for h in range(HPG):
        qaug_scr[h * TQ:(h + 1) * TQ, LANES:LANES + n_blk] = bias

    def online(s, v, carry, keep=None):
        m_prev, l_prev, acc = carry
        m_new = jnp.maximum(m_prev, jnp.max(s, axis=-1, keepdims=True))
        alpha = jnp.exp(m_prev - m_new)
        p = jnp.exp(s - m_new)
        if keep is not None:
            p = jnp.where(keep, p, 0.0)
        l_new = alpha * l_prev + jnp.sum(p, axis=-1, keepdims=True)
        return m_new, l_new, alpha * acc + _dot(p.astype(v.dtype), v)

    init = (jnp.full((m_rows, 1), -jnp.inf, F32), jnp.zeros((m_rows, 1), F32),
            jnp.zeros((m_rows, HEAD_DIM), F32))
    k_io = lax.broadcasted_iota(jnp.int32, (1, TK), 1)
    diag = t0 // TK

    def sel_tile(kt, carry, causal):
        k0 = pl.multiple_of(kt * TK, TK)
        s = _dot(qaug_scr[...], kts_ref[:, pl.ds(k0, TK)])
        if causal:
            s = jnp.where(k0 + k_io <= row_t, s, -MASK_BIG)
        return online(s, vs_ref[pl.ds(k0, TK), :], carry)

    carry = lax.fori_loop(0, diag, functools.partial(sel_tile, causal=False), init)
    _, l_s, acc_s = sel_tile(diag, carry, causal=True)

    def win_tile(kt, carry):
        k0 = pl.multiple_of(kt * TK, TK)
        kpos = k0 + k_io
        keep = (kpos <= row_t) & (kpos > row_t - WINDOW)
        s = jnp.where(keep, _dot(q, ktw_ref[:, pl.ds(k0, TK)]), -MASK_BIG)
        return online(s, vw_ref[pl.ds(k0, TK), :], carry, keep)

    _, l_w, acc_w = lax.fori_loop(jnp.maximum(diag - WINDOW // TK, 0), diag + 1, win_tile, init)

    g = g_ref[...].reshape(m_rows, 3)
    out = g[:, 0:1] * o_c + g[:, 1:2] * (acc_s / l_s) + g[:, 2:3] * (acc_w / l_w)
    o_ref[...] = out.reshape(HPG, TQ, HEAD_DIM).astype(o_ref.dtype)


def _nsa(q, kts, vs, ktw, vw, kct, vc, c2s, gates):
    bsz, ng, _, seq, _ = q.shape
    n_cmp = kct.shape[-1]
    n_blk = c2s.shape[0]
    per_group = lambda *shape: pl.BlockSpec((None, None) + shape, lambda bi, gi, qi: (bi, gi) + (0,) * len(shape))
    q_tile = lambda last: pl.BlockSpec((None, None, HPG, TQ, last), lambda bi, gi, qi: (bi, gi, 0, qi, 0))
    return pl.pallas_call(
        _nsa_kernel,
        grid=(bsz, ng, seq // TQ),
        in_specs=[
            q_tile(HEAD_DIM),
            per_group(2 * LANES, seq), per_group(seq, HEAD_DIM),
            per_group(HEAD_DIM, seq), per_group(seq, HEAD_DIM),
            per_group(HEAD_DIM, n_cmp), per_group(n_cmp, HEAD_DIM),
            pl.BlockSpec((n_blk, n_cmp), lambda bi, gi, qi: (0, 0)),
            q_tile(3),
        ],
        out_specs=q_tile(HEAD_DIM),
        out_shape=jax.ShapeDtypeStruct((bsz, ng, HPG, seq, HEAD_DIM), q.dtype),
        scratch_shapes=[pltpu.VMEM((HPG * TQ, 2 * LANES), q.dtype)],
        compiler_params=_params("parallel", "parallel", "arbitrary"),
        name="nsa",
    )(q, kts, vs, ktw, vw, kct, vc, c2s, gates)


def _merge_kernel(x_ref, hc_ref, ao_ref, gc_ref, gn_ref, wc_ref, wn_ref, wo_ref, o_ref):
    y_conv = _dot(hc_ref[...], wc_ref[...])
    y_nsa = _dot(ao_ref[...], wn_ref[...])
    mixed = gc_ref[...] * y_conv + gn_ref[...] * y_nsa
    o_ref[...] = x_ref[...] + _dot(mixed.astype(wo_ref.dtype), wo_ref[...])


def _merge(x, hc, ao, gates, wc, wn, wo):
    n = x.shape[0]
    row = lambda j: pl.BlockSpec((TM_OUT, D_MODEL), lambda i: (i, j))
    w_spec = pl.BlockSpec((D_MODEL, D_MODEL), lambda i: (0, 0))
    return pl.pallas_call(
        _merge_kernel,
        grid=(n // TM_OUT,),
        in_specs=[row(0), row(0), row(0), row(0), row(1), w_spec, w_spec, w_spec],
        out_specs=row(0),
        out_shape=jax.ShapeDtypeStruct((n, D_MODEL), F32),
        compiler_params=_params("parallel"),
        name="merge_out",
    )(x, hc, ao, gates, gates, wc, wn, wo)


def _ple_kernel(x_ref, p_ref, g_ref, gp_ref, wg_ref, wp_ref, o_ref):
    x = x_ref[...]
    e = _rms(_dot(p_ref[...].astype(wp_ref.dtype), wp_ref[...]), gp_ref[...])
    gate = jax.nn.sigmoid(_dot(_rms(x, g_ref[...]).astype(wg_ref.dtype), wg_ref[...]))
    o_ref[...] = x + gate * e


def _ple(x, p, g, g_post, w_gate, w_proj):
    n = x.shape[0]
    vec = pl.BlockSpec((1, D_MODEL), lambda i: (0, 0))
    return pl.pallas_call(
        _ple_kernel,
        grid=(n // TM_OUT,),
        in_specs=[
            pl.BlockSpec((TM_OUT, D_MODEL), lambda i: (i, 0)),
            pl.BlockSpec((TM_OUT, PLE_DIM), lambda i: (i, 0)),
            vec, vec,
            pl.BlockSpec((D_MODEL, D_MODEL), lambda i: (0, 0)),
            pl.BlockSpec((PLE_DIM, D_MODEL), lambda i: (0, 0)),
        ],
        out_specs=pl.BlockSpec((TM_OUT, D_MODEL), lambda i: (i, 0)),
        out_shape=jax.ShapeDtypeStruct((n, D_MODEL), F32),
        compiler_params=_params("parallel"),
        name="ple",
    )(x, p, g, g_post, w_gate, w_proj)


def _rope_tables(pos, width):
    half = ROPE_DIM // 2
    inv = ROPE_THETA ** (-jnp.arange(half, dtype=F32) / half)
    ang = pos.astype(F32)[:, None] * inv[None, :]
    cos, sin = jnp.cos(ang), jnp.sin(ang)
    n = pos.shape[0]
    rest = HEAD_DIM - ROPE_DIM
    seg_cos = jnp.concatenate([cos, cos, jnp.ones((n, rest), F32)], axis=1)
    seg_lo = jnp.concatenate([-sin, jnp.zeros((n, HEAD_DIM - half), F32)], axis=1)
    seg_hi = jnp.concatenate([jnp.zeros((n, half), F32), sin, jnp.zeros((n, rest), F32)], axis=1)
    reps = width // HEAD_DIM
    return tuple(jnp.tile(t, (1, reps)) for t in (seg_cos, seg_lo, seg_hi))


def kernel(x, p, ffn1_norm, ffn1_w_in, ffn1_w_out, mix_norm, w_in, b_in, conv_w, conv_b, conv_ln_g, conv_ln_b, conv_w_out, q_norm, k_norm, cmp_pos, cmp_w1, cmp_b1, cmp_w2, nsa_w_out, w_out, ffn2_norm, ffn2_w_in, ffn2_w_out, ple_norm, ple_w_gate, ple_w_proj, ple_post_norm):
    bsz, seq, _ = x.shape
    depth = w_in.shape[0]
    n_rows = bsz * seq
    n_sub = seq // CMP_STRIDE
    n_blk = seq // SEL_LEN
    mx = MXU_DTYPE
    row = lambda v: v[:, None, :]

    def kv_cols(j):
        return slice(OFF_KV + j * KV_W, OFF_KV + (j + 1) * KV_W)

    def regroup(w):
        glu = w[..., 0:OFF_Q]
        qk = jnp.concatenate([w[..., OFF_Q:OFF_KV], w[..., kv_cols(2)], w[..., kv_cols(4)]], axis=-1)
        plain = jnp.concatenate([w[..., kv_cols(0)], w[..., kv_cols(1)], w[..., kv_cols(3)], w[..., kv_cols(5)]], axis=-1)
        pad = jnp.zeros(w.shape[:-1] + (TN - (OFF_MERGE - OFF_GATE),), w.dtype)
        sig = jnp.concatenate([w[..., OFF_MERGE:N_IN], w[..., OFF_GATE:OFF_MERGE], pad], axis=-1)
        return glu, qk, plain, sig

    w_glu, w_qk, w_plain, w_sig = (w.astype(mx) for w in regroup(w_in))
    b_glu, b_qk, b_plain, b_sig = (row(b) for b in regroup(b_in))

    q_gain = jnp.tile(q_norm, (1, HPG)) * (HEAD_DIM ** -0.5)
    head_gain = jnp.concatenate(
        [jnp.tile(q_gain, (1, N_HEADS // HPG)), jnp.tile(k_norm[:, 1], (1, N_KV)), jnp.tile(k_norm[:, 2], (1, N_KV))],
        axis=-1)[:, None, :]
    cmp_gain = jnp.concatenate([k_norm[:, 0], jnp.zeros((depth, LANES - HEAD_DIM), F32)], axis=-1)[:, None, :]
    w2_pad = jnp.concatenate([cmp_w2, jnp.zeros(cmp_w2.shape[:-1] + (LANES - HEAD_DIM,), F32)], axis=-1).astype(mx)
    sub_w = CMP_STRIDE * HEAD_DIM
    pos_flat = cmp_pos.reshape(depth, 2, 2, 1, sub_w)

    layers = dict(
        ffn1_norm=row(ffn1_norm), ffn1_w_in=ffn1_w_in.astype(mx), ffn1_w_out=ffn1_w_out.astype(mx),
        mix_norm=row(mix_norm), w_glu=w_glu, w_qk=w_qk, w_plain=w_plain, w_sig=w_sig,
        b_glu=b_glu, b_qk=b_qk, b_plain=b_plain, b_sig=b_sig, head_gain=head_gain,
        conv_w=conv_w, conv_b=row(conv_b), conv_ln_g=row(conv_ln_g), conv_ln_b=row(conv_ln_b),
        conv_w_out=conv_w_out.astype(mx), cmp_gain=cmp_gain, pos_flat=pos_flat,
        cmp_w1=cmp_w1.astype(mx), cmp_b1=cmp_b1[:, :, None, :], cmp_w2=w2_pad,
        nsa_w_out=nsa_w_out.astype(mx), w_out=w_out.astype(mx),
        ffn2_norm=row(ffn2_norm), ffn2_w_in=ffn2_w_in.astype(mx), ffn2_w_out=ffn2_w_out.astype(mx),
        ple_norm=row(ple_norm), ple_w_gate=ple_w_gate.astype(mx), ple_w_proj=ple_w_proj.astype(mx),
        ple_post_norm=row(ple_post_norm), p=p.reshape(depth, n_rows, PLE_DIM),
    )

    tok_cos, tok_lo, tok_hi = _rope_tables(jnp.arange(seq), TN)
    cmp_cos, cmp_lo, cmp_hi = _rope_tables(jnp.arange(n_sub) * CMP_STRIDE + CMP_LEN - 1, LANES)
    seg = jnp.arange(TN) // HEAD_DIM
    ones_bd = (seg[:, None] == seg[None, :]).astype(mx)
    ci = jnp.arange(n_sub)[None, :]
    sj = jnp.arange(n_blk)[:, None]
    c2s = ((ci * CMP_STRIDE < (sj + 1) * SEL_LEN) & (ci * CMP_STRIDE + CMP_LEN > sj * SEL_LEN)).astype(mx)
    blk_of_key = (jnp.arange(n_blk)[:, None] == (jnp.arange(seq) // SEL_LEN)[None, :]).astype(mx)
    k_aug_tail = jnp.concatenate([jnp.zeros((LANES - HEAD_DIM, seq), mx), blk_of_key,
                                  jnp.zeros((LANES - n_blk, seq), mx)], axis=0)
    k_aug_tail = jnp.broadcast_to(k_aug_tail, (bsz, N_KV) + k_aug_tail.shape)

    def by_group(a):
        return a.reshape(bsz, seq, N_KV, HEAD_DIM).transpose(0, 2, 1, 3)

    def by_group_t(a):
        return a.reshape(bsz, seq, N_KV, HEAD_DIM).transpose(0, 2, 3, 1)

    def sub_blocks(a):
        a = a.reshape(bsz, n_sub, CMP_STRIDE, N_KV, HEAD_DIM).transpose(0, 3, 1, 2, 4)
        return a.reshape(bsz, N_KV, n_sub, sub_w)

    def layer(xf, lp):
        xf = _ffn(xf, lp["ffn1_norm"], lp["ffn1_w_in"], lp["ffn1_w_out"])

        h_glu = _glu_proj(xf, lp["mix_norm"], lp["w_glu"], lp["b_glu"])
        qk = _qk_proj(xf, lp["mix_norm"], lp["w_qk"], lp["b_qk"], lp["head_gain"], ones_bd,
                      tok_cos, tok_lo, tok_hi, seq)
        plain = _act_proj(xf, lp["mix_norm"], lp["w_plain"], lp["b_plain"], sigmoid=False, name="kv_proj")
        sig = _act_proj(xf, lp["mix_norm"], lp["w_sig"], lp["b_sig"], sigmoid=True, name="gate_proj")

        hc = _conv_module(h_glu.reshape(bsz, seq, CONV_DIM), lp["conv_w"], lp["conv_b"],
                          lp["conv_ln_g"], lp["conv_ln_b"], mx).reshape(n_rows, CONV_DIM)

        comp = []
        for j, is_key in ((0, True), (1, False)):
            pos = lp["pos_flat"][j]
            comp.append(_compress(
                sub_blocks(plain[:, j * KV_W:(j + 1) * KV_W]), pos[0], pos[1],
                lp["cmp_w1"][j, :sub_w], lp["cmp_w1"][j, sub_w:], lp["cmp_b1"][j], lp["cmp_w2"][j],
                lp["cmp_gain"], cmp_cos, cmp_lo, cmp_hi, is_key=is_key, out_dtype=mx))
        kct = comp[0].transpose(0, 1, 3, 2)
        vc = comp[1]

        q = qk[:, :N_HEADS * HEAD_DIM].reshape(bsz, seq, N_KV, HPG, HEAD_DIM).transpose(0, 2, 3, 1, 4)
        off = N_HEADS * HEAD_DIM
        kts = jnp.concatenate([by_group_t(qk[:, off:off + KV_W]), k_aug_tail], axis=2)
        ktw = by_group_t(qk[:, off + KV_W:off + 2 * KV_W])
        vs = by_group(plain[:, 2 * KV_W:3 * KV_W]).astype(mx)
        vw = by_group(plain[:, 3 * KV_W:4 * KV_W]).astype(mx)
        gates = sig[:, 2 * D_MODEL:2 * D_MODEL + 3 * N_HEADS]
        gates = gates.reshape(bsz, seq, N_KV, HPG, 3).transpose(0, 2, 3, 1, 4)
        ao = _nsa(q, kts, vs, ktw, vw, kct, vc, c2s, gates)
        ao = ao.transpose(0, 3, 1, 2, 4).reshape(n_rows, N_HEADS * HEAD_DIM)

        xf = _merge(xf, hc, ao, sig, lp["conv_w_out"], lp["nsa_w_out"], lp["w_out"])
        xf = _ffn(xf, lp["ffn2_norm"], lp["ffn2_w_in"], lp["ffn2_w_out"])
        xf = _ple(xf, lp["p"], lp["ple_norm"], lp["ple_post_norm"], lp["ple_w_gate"], lp["ple_w_proj"])
        return xf, None

    out, _ = lax.scan(layer, x.reshape(n_rows, D_MODEL), layers)
    return out.reshape(bsz, seq, D_MODEL)
```

```python
import functools

import jax
import jax.numpy as jnp
from jax import lax
from jax.experimental import pallas as pl
from jax.experimental.pallas import tpu as pltpu

D_MODEL = 1024
PLE_DIM = 256
D_FF = 2816
CONV_DIM = D_MODEL
CONV_K = 31
N_HEADS = 16
N_KV = 4
HPG = N_HEADS // N_KV
HEAD_DIM = 64
ROPE_DIM = HEAD_DIM // 4
ROPE_THETA = 500000.0
CMP_LEN = 32
CMP_STRIDE = 16
CMP_HID = 256
SEL_LEN = 64
N_SEL = 16
WINDOW = 512
NEG = -1e30
EPS = 1e-6
KV_W = N_KV * HEAD_DIM
OFF_Q = 2 * CONV_DIM
OFF_KV = OFF_Q + N_HEADS * HEAD_DIM
OFF_GATE = OFF_KV + 6 * KV_W
OFF_MERGE = OFF_GATE + 3 * N_HEADS
N_IN = OFF_MERGE + 2 * D_MODEL

MXU_DTYPE = jnp.bfloat16
F32 = jnp.float32
LANES = 128
VMEM_LIMIT = 56 * 1024 * 1024

TM = 1024
TN = 256
TF = 256
TM_OUT = 512
TT = 256
HALO = 32
CONV_ROWS = 32
CONV_COLS = 256
TQ = 128
TK_SEL = 256
SEL_GROUP = 4
MASK_BIG = 2.0 ** 100
LOG2_E = 1.4426950408889634
V_ROWS = HEAD_DIM + 16


def _params(*sem):
    return pltpu.CompilerParams(dimension_semantics=sem, vmem_limit_bytes=VMEM_LIMIT)


def _rms(x, g):
    return x * lax.rsqrt(jnp.mean(x * x, axis=-1, keepdims=True) + EPS) * g


def _dot(a, b):
    return jnp.dot(a, b, preferred_element_type=F32)


def _col_reduce(reduce_fn, combine_fn, s, parts=8):
    n = s.shape[0] // parts
    chunks = [reduce_fn(s[i * n:(i + 1) * n], axis=0, keepdims=True) for i in range(parts)]
    while len(chunks) > 1:
        chunks = [combine_fn(chunks[i], chunks[i + 1]) for i in range(0, len(chunks), 2)]
    return chunks[0]


def _ffn_kernel(x_ref, g_ref, wa_ref, wb_ref, wo_ref, o_ref, h_scr, acc_scr):
    f = pl.program_id(1)

    @pl.when(f == 0)
    def _():
        h_scr[...] = _rms(x_ref[...], g_ref[...]).astype(h_scr.dtype)
        acc_scr[...] = jnp.zeros_like(acc_scr)

    h = h_scr[...]
    a = _dot(h, wa_ref[...])
    b = _dot(h, wb_ref[...])
    act = (a * jax.nn.sigmoid(a) * b).astype(wo_ref.dtype)
    acc_scr[...] += _dot(act, wo_ref[...])

    @pl.when(f == pl.num_programs(1) - 1)
    def _():
        o_ref[...] = x_ref[...] + 0.5 * acc_scr[...]


def _ffn(x, g, w_in, w_out):
    n = x.shape[0]
    nf = D_FF // TF
    return pl.pallas_call(
        _ffn_kernel,
        grid=(n // TM, nf),
        in_specs=[
            pl.BlockSpec((TM, D_MODEL), lambda i, f: (i, 0)),
            pl.BlockSpec((1, D_MODEL), lambda i, f: (0, 0)),
            pl.BlockSpec((D_MODEL, TF), lambda i, f: (0, f)),
            pl.BlockSpec((D_MODEL, TF), lambda i, f: (0, f + nf)),
            pl.BlockSpec((TF, D_MODEL), lambda i, f: (f, 0)),
        ],
        out_specs=pl.BlockSpec((TM, D_MODEL), lambda i, f: (i, 0)),
        out_shape=jax.ShapeDtypeStruct((n, D_MODEL), F32),
        scratch_shapes=[pltpu.VMEM((TM, D_MODEL), w_in.dtype), pltpu.VMEM((TM, D_MODEL), F32)],
        compiler_params=_params("parallel", "arbitrary"),
        name="ffn",
    )(x, g, w_in, w_in, w_out)


def _norm_to_scratch(x_ref, g_ref, h_scr):
    @pl.when(pl.program_id(1) == 0)
    def _():
        h_scr[...] = _rms(x_ref[...], g_ref[...]).astype(h_scr.dtype)


def _glu_kernel(x_ref, g_ref, wa_ref, wg_ref, ba_ref, bg_ref, o_ref, h_scr):
    _norm_to_scratch(x_ref, g_ref, h_scr)
    h = h_scr[...]
    a = _dot(h, wa_ref[...]) + ba_ref[...]
    gate = _dot(h, wg_ref[...]) + bg_ref[...]
    o_ref[...] = a * jax.nn.sigmoid(gate)


def _act_proj_kernel(x_ref, g_ref, w_ref, b_ref, o_ref, h_scr, *, sigmoid):
    _norm_to_scratch(x_ref, g_ref, h_scr)
    y = _dot(h_scr[...], w_ref[...]) + b_ref[...]
    o_ref[...] = (jax.nn.sigmoid(y) if sigmoid else y).astype(o_ref.dtype)


def _head_norm_rope(y, ones_bd, gain, cos, sin_lo, sin_hi):
    ss = _dot((y * y).astype(ones_bd.dtype), ones_bd)
    yn = y * lax.rsqrt(ss * (1.0 / HEAD_DIM) + EPS) * gain
    width = y.shape[-1]
    half = ROPE_DIM // 2
    return (yn * cos + pltpu.roll(yn, width - half, 1) * sin_lo + pltpu.roll(yn, half, 1) * sin_hi)


def _qk_proj_kernel(x_ref, g_ref, w_ref, b_ref, hg_ref, bd_ref, cos_ref, slo_ref, shi_ref, o_ref, h_scr):
    _norm_to_scratch(x_ref, g_ref, h_scr)
    y = _dot(h_scr[...], w_ref[...]) + b_ref[...]
    out = _head_norm_rope(y, bd_ref[...], hg_ref[...], cos_ref[...], slo_ref[...], shi_ref[...])
    o_ref[...] = out.astype(o_ref.dtype)


def _row_col_specs(n_w):
    x_spec = pl.BlockSpec((TM, D_MODEL), lambda i, j: (i, 0))
    g_spec = pl.BlockSpec((1, D_MODEL), lambda i, j: (0, 0))
    w_spec = pl.BlockSpec((D_MODEL, TN), lambda i, j: (0, j))
    b_spec = pl.BlockSpec((1, TN), lambda i, j: (0, j))
    o_spec = pl.BlockSpec((TM, TN), lambda i, j: (i, j))
    return x_spec, g_spec, w_spec, b_spec, o_spec


def _glu_proj(x, g, w, b):
    n = x.shape[0]
    nj = CONV_DIM // TN
    x_spec, g_spec, w_spec, b_spec, o_spec = _row_col_specs(w.shape[1])
    w2_spec = pl.BlockSpec((D_MODEL, TN), lambda i, j: (0, j + nj))
    b2_spec = pl.BlockSpec((1, TN), lambda i, j: (0, j + nj))
    return pl.pallas_call(
        _glu_kernel,
        grid=(n // TM, nj),
        in_specs=[x_spec, g_spec, w_spec, w2_spec, b_spec, b2_spec],
        out_specs=o_spec,
        out_shape=jax.ShapeDtypeStruct((n, CONV_DIM), F32),
        scratch_shapes=[pltpu.VMEM((TM, D_MODEL), w.dtype)],
        compiler_params=_params("parallel", "arbitrary"),
        name="glu_proj",
    )(x, g, w, w, b, b)


def _act_proj(x, g, w, b, *, sigmoid, name):
    n = x.shape[0]
    x_spec, g_spec, w_spec, b_spec, o_spec = _row_col_specs(w.shape[1])
    return pl.pallas_call(
        functools.partial(_act_proj_kernel, sigmoid=sigmoid),
        grid=(n // TM, w.shape[1] // TN),
        in_specs=[x_spec, g_spec, w_spec, b_spec],
        out_specs=o_spec,
        out_shape=jax.ShapeDtypeStruct((n, w.shape[1]), F32),
        scratch_shapes=[pltpu.VMEM((TM, D_MODEL), w.dtype)],
        compiler_params=_params("parallel", "arbitrary"),
        name=name,
    )(x, g, w, b)


def _qk_proj(x, g, w, b, head_gain, ones_bd, cos, sin_lo, sin_hi, seq):
    n = x.shape[0]
    x_spec, g_spec, w_spec, b_spec, o_spec = _row_col_specs(w.shape[1])
    t_tiles = seq // TM
    tab_spec = pl.BlockSpec((TM, TN), lambda i, j: (i % t_tiles, 0))
    return pl.pallas_call(
        _qk_proj_kernel,
        grid=(n // TM, w.shape[1] // TN),
        in_specs=[x_spec, g_spec, w_spec, b_spec, b_spec,
                  pl.BlockSpec((TN, TN), lambda i, j: (0, 0)), tab_spec, tab_spec, tab_spec],
        out_specs=o_spec,
        out_shape=jax.ShapeDtypeStruct((n, w.shape[1]), w.dtype),
        scratch_shapes=[pltpu.VMEM((TM, D_MODEL), w.dtype)],
        compiler_params=_params("parallel", "arbitrary"),
        name="qk_proj",
    )(x, g, w, b, head_gain, ones_bd, cos, sin_lo, sin_hi)


def _conv_kernel(cur_ref, halo_ref, w_ref, b_ref, lg_ref, lb_ref, o_ref, sh_scr, conv_scr):
    t = pl.program_id(1)
    ext = TT + HALO - 8
    sh_scr[0, 0:HALO, :] = jnp.where(t == 0, 0.0, halo_ref[...])
    sh_scr[0, HALO:HALO + TT, :] = cur_ref[...]
    for r in range(1, 8):
        sh_scr[r, 0:ext, :] = sh_scr[0, r:r + ext, :]

    lead = HALO - (CONV_K - 1)

    def row_chunk(ri, carry):
        r0 = pl.multiple_of(ri * CONV_ROWS, CONV_ROWS)
        for c0 in range(0, CONV_DIM, CONV_COLS):
            acc = jnp.broadcast_to(b_ref[:, c0:c0 + CONV_COLS], (CONV_ROWS, CONV_COLS))
            for k in range(CONV_K):
                off = k + lead
                base = pl.multiple_of(r0 + (off // 8) * 8, 8)
                acc = acc + w_ref[k:k + 1, c0:c0 + CONV_COLS] * sh_scr[off % 8, pl.ds(base, CONV_ROWS),
                                                                       c0:c0 + CONV_COLS]
            conv_scr[pl.ds(r0, CONV_ROWS), c0:c0 + CONV_COLS] = acc
        return carry

    lax.fori_loop(0, TT // CONV_ROWS, row_chunk, 0)

    h = conv_scr[...]
    mu = jnp.mean(h, axis=-1, keepdims=True)
    var = jnp.mean(jnp.square(h - mu), axis=-1, keepdims=True)
    y = (h - mu) * lax.rsqrt(var + EPS) * lg_ref[...] + lb_ref[...]
    o_ref[...] = (y * jax.nn.sigmoid(y)).astype(o_ref.dtype)


def _conv_module(h, w, b, ln_g, ln_b, out_dtype):
    bsz, seq, _ = h.shape
    halo_per_tile = TT // HALO
    vec = pl.BlockSpec((1, CONV_DIM), lambda bi, t: (0, 0))
    return pl.pallas_call(
        _conv_kernel,
        grid=(bsz, seq // TT),
        in_specs=[
            pl.BlockSpec((None, TT, CONV_DIM), lambda bi, t: (bi, t, 0)),
            pl.BlockSpec((None, HALO, CONV_DIM), lambda bi, t: (bi, jnp.maximum(t * halo_per_tile - 1, 0), 0)),
            pl.BlockSpec((CONV_K, CONV_DIM), lambda bi, t: (0, 0)),
            vec, vec, vec,
        ],
        out_specs=pl.BlockSpec((None, TT, CONV_DIM), lambda bi, t: (bi, t, 0)),
        out_shape=jax.ShapeDtypeStruct((bsz, seq, CONV_DIM), out_dtype),
        scratch_shapes=[pltpu.VMEM((8, TT + HALO, CONV_DIM), F32), pltpu.VMEM((TT, CONV_DIM), F32)],
        compiler_params=_params("parallel", "parallel"),
        name="conv_module",
    )(h, h, w, b, ln_g, ln_b)


def _compress_kernel(a_ref, plo_ref, phi_ref, w1a_ref, w1b_ref, b1_ref, w2_ref,
                     hg_ref, cos_ref, slo_ref, shi_ref, o_ref, *, is_key):
    a = a_ref[...]
    n = a.shape[0]
    p = _dot((a + plo_ref[...]).astype(w1a_ref.dtype), w1a_ref[...])
    q = _dot((a + phi_ref[...]).astype(w1b_ref.dtype), w1b_ref[...])
    hid = p + pltpu.roll(q, n - 1, 0) + b1_ref[...]
    hid = jax.nn.gelu(hid, approximate=True)
    c = _dot(hid.astype(w2_ref.dtype), w2_ref[...])
    if is_key:
        ss = jnp.sum(c * c, axis=-1, keepdims=True)
        cn = c * lax.rsqrt(ss * (1.0 / HEAD_DIM) + EPS) * hg_ref[...]
        half = ROPE_DIM // 2
        c = cn * cos_ref[...] + pltpu.roll(cn, LANES - half, 1) * slo_ref[...] + pltpu.roll(cn, half, 1) * shi_ref[...]
    o_ref[...] = c[:, :HEAD_DIM].astype(o_ref.dtype)


def _compress(a, pos_lo, pos_hi, w1a, w1b, b1, w2, head_gain, cos, sin_lo, sin_hi, *, is_key, out_dtype):
    bsz, ng, n_sub, width = a.shape
    full = lambda *shape: pl.BlockSpec(shape, lambda bi, gi: (0,) * len(shape))
    return pl.pallas_call(
        functools.partial(_compress_kernel, is_key=is_key),
        grid=(bsz, ng),
        in_specs=[
            pl.BlockSpec((None, None, n_sub, width), lambda bi, gi: (bi, gi, 0, 0)),
            full(1, width), full(1, width), full(width, CMP_HID), full(width, CMP_HID),
            full(1, CMP_HID), full(CMP_HID, LANES), full(1, LANES),
            full(n_sub, LANES), full(n_sub, LANES), full(n_sub, LANES),
        ],
        out_specs=pl.BlockSpec((None, None, n_sub, HEAD_DIM), lambda bi, gi: (bi, gi, 0, 0)),
        out_shape=jax.ShapeDtypeStruct((bsz, ng, n_sub, HEAD_DIM), out_dtype),
        compiler_params=_params("parallel", "parallel"),
        name="compress_k" if is_key else "compress_v",
    )(a, pos_lo, pos_hi, w1a, w1b, b1, w2, head_gain, cos, sin_lo, sin_hi)


def _nsa_kernel(qt_ref, ks_ref, vst_ref, kw_ref, vwt_ref, kc_ref, vct_ref, c2s_ref, g_ref, o_ref, qaug_scr):
    m_cols = HPG * TQ
    t0 = pl.program_id(2) * TQ
    qt = jnp.concatenate([qt_ref[h] for h in range(HPG)], axis=1)
    col_t = t0 + (lax.broadcasted_iota(jnp.int32, (1, m_cols), 1) & (TQ - 1))

    n_cmp = kc_ref.shape[0]
    s = _dot(kc_ref[...], qt)
    cmp_end = lax.broadcasted_iota(jnp.int32, (n_cmp, 1), 0) * CMP_STRIDE + (CMP_LEN - 1)
    cmask = cmp_end <= col_t
    s = jnp.where(cmask, s, NEG)
    p = jnp.exp2(s - jnp.maximum(_col_reduce(jnp.max, jnp.maximum, s), 0.1 * NEG))
    l = _col_reduce(jnp.sum, jnp.add, p)
    pc = p * (1.0 / jnp.where(l > 0.0, l, 1.0))
    o_c = _dot(vct_ref[...], pc.astype(vct_ref.dtype))

    tq_io = lax.broadcasted_iota(jnp.int32, (TQ, 1), 0)
    w0 = pl.multiple_of(jnp.maximum(t0 - WINDOW, 0), TQ)
    s = _dot(kw_ref[pl.ds(w0, WINDOW + TQ), :], qt)
    s = jnp.concatenate([jnp.where(w0 + tq_io > col_t - WINDOW, s[0:TQ], -MASK_BIG), s[TQ:]], axis=0)
    s = jnp.where(w0 + lax.broadcasted_iota(jnp.int32, (WINDOW + TQ, 1), 0) <= col_t, s, -MASK_BIG)
    p = jnp.exp2(s - _col_reduce(jnp.max, jnp.maximum, s))
    acc_w = _dot(vwt_ref[:, pl.ds(w0, WINDOW + TQ)], p.astype(vwt_ref.dtype))
    acc_w, l_w = acc_w[0:HEAD_DIM], acc_w[HEAD_DIM:HEAD_DIM + 1]

    pc_sum = pc[:, 0:TQ] + pc[:, TQ:2 * TQ] + pc[:, 2 * TQ:3 * TQ] + pc[:, 3 * TQ:4 * TQ]
    c2s = c2s_ref[...]
    hi = pc_sum.astype(c2s.dtype)
    lo = (pc_sum - hi.astype(F32)).astype(c2s.dtype)
    imp = _dot(c2s, hi) + _dot(c2s, lo)
    n_blk = imp.shape[0]
    j_io = lax.broadcasted_iota(jnp.int32, (n_blk, TQ), 0)
    t_io = t0 + lax.broadcasted_iota(jnp.int32, (n_blk, TQ), 1)
    cur = t_io // SEL_LEN
    forced = (j_io == 0) | (j_io == cur) | (j_io == cur - 1)
    valid = j_io * SEL_LEN <= t_io
    imp = jnp.where(valid, jnp.where(forced, 1e4, imp), -1.0)

    sel = jnp.zeros((n_blk, TQ), F32)
    j_f = j_io.astype(F32)
    for _ in range(min(N_SEL, n_blk)):
        mx = jnp.max(imp, axis=0, keepdims=True)
        first = jnp.min(jnp.where(imp == mx, j_f, float(n_blk)), axis=0, keepdims=True)
        pick = j_f == first
        sel = jnp.where(pick & (mx >= 0.0), 1.0, sel)
        imp = jnp.where(pick, -3e38, imp)
    bias = ((jnp.where(j_io == cur, 0.0, sel) - 1.0) * MASK_BIG).astype(qaug_scr.dtype)

    qaug_scr[0:HEAD_DIM, :] = qt
    qaug_scr[HEAD_DIM:LANES, :] = jnp.zeros((LANES - HEAD_DIM, m_cols), qaug_scr.dtype)
    if n_blk < LANES:
        qaug_scr[LANES + n_blk:, :] = jnp.zeros((LANES - n_blk, m_cols), qaug_scr.dtype)
    for h in range(HPG):
        qaug_scr[LANES:LANES + n_blk, h * TQ:(h + 1) * TQ] = bias

    def online(s, vt, carry):
        m_prev, acc = carry
        m_new = jnp.maximum(m_prev, _col_reduce(jnp.max, jnp.maximum, s))
        alpha = jnp.exp2(m_prev - m_new)
        p = jnp.exp2(s - m_new)
        return m_new, alpha * acc + _dot(vt, p.astype(vt.dtype))

    def finish(carry):
        _, acc = carry
        return acc[0:HEAD_DIM], acc[HEAD_DIM:HEAD_DIM + 1]

    init = (jnp.full((1, m_cols), -jnp.inf, F32), jnp.zeros((V_ROWS, m_cols), F32))

    t0a = pl.multiple_of(t0, TQ)
    own = ((t0 + tq_io) // SEL_LEN == col_t // SEL_LEN) & (t0 + tq_io <= col_t)
    s = jnp.where(own, _dot(ks_ref[pl.ds(t0a, TQ), 0:HEAD_DIM], qt), -MASK_BIG)
    carry = online(s, vst_ref[:, pl.ds(t0a, TQ)], init)

    def sel_group(i, carry):
        starts = [pl.multiple_of((SEL_GROUP * i + part) * TK_SEL, TK_SEL) for part in range(SEL_GROUP)]
        scores = [_dot(ks_ref[pl.ds(k0, TK_SEL), :], qaug_scr[...]) for k0 in starts]
        for k0, s in zip(starts, scores):
            carry = online(s, vst_ref[:, pl.ds(k0, TK_SEL)], carry)
        return carry

    acc_s, l_s = finish(lax.fori_loop(0, t0 // (SEL_GROUP * TK_SEL) + 1, sel_group, carry))

    gate = lambda c: jnp.concatenate([g_ref[c, h:h + 1, :] for h in range(HPG)], axis=1)
    out = gate(0) * o_c + (gate(1) * (1.0 / l_s)) * acc_s + (gate(2) * (1.0 / l_w)) * acc_w
    for h in range(HPG):
        o_ref[h] = out[:, h * TQ:(h + 1) * TQ].astype(o_ref.dtype)


def _nsa(qt, ks, vst, kw, vwt, kc, vct, c2s, gates):
    bsz, ng, _, _, seq = qt.shape
    n_cmp = kc.shape[2]
    n_blk = c2s.shape[0]
    per_group = lambda *shape: pl.BlockSpec((None, None) + shape, lambda bi, gi, qi: (bi, gi) + (0,) * len(shape))
    q_tile = lambda *lead: pl.BlockSpec((None, None) + lead + (TQ,), lambda bi, gi, qi: (bi, gi) + (0,) * len(lead) + (qi,))
    return pl.pallas_call(
        _nsa_kernel,
        grid=(bsz, ng, seq // TQ),
        in_specs=[
            q_tile(HPG, HEAD_DIM),
            per_group(seq, 2 * LANES), per_group(V_ROWS, seq),
            per_group(seq, HEAD_DIM), per_group(V_ROWS, seq),
            per_group(n_cmp, HEAD_DIM), per_group(HEAD_DIM, n_cmp),
            pl.BlockSpec((n_blk, n_cmp), lambda bi, gi, qi: (0, 0)),
            q_tile(3, HPG),
        ],
        out_specs=q_tile(HPG, HEAD_DIM),
        out_shape=jax.ShapeDtypeStruct((bsz, ng, HPG, HEAD_DIM, seq), qt.dtype),
        scratch_shapes=[pltpu.VMEM((2 * LANES, HPG * TQ), qt.dtype)],
        compiler_params=_params("parallel", "parallel", "arbitrary"),
        name="nsa",
    )(qt, ks, vst, kw, vwt, kc, vct, c2s, gates)


def _merge_kernel(x_ref, hc_ref, ao_ref, gc_ref, gn_ref, wc_ref, wn_ref, wo_ref, o_ref):
    y_conv = _dot(hc_ref[...], wc_ref[...])
    y_nsa = _dot(ao_ref[...], wn_ref[...])
    mixed = gc_ref[...] * y_conv + gn_ref[...] * y_nsa
    o_ref[...] = x_ref[...] + _dot(mixed.astype(wo_ref.dtype), wo_ref[...])


def _merge(x, hc, ao, gates, wc, wn, wo):
    n = x.shape[0]
    row = lambda j: pl.BlockSpec((TM_OUT, D_MODEL), lambda i: (i, j))
    w_spec = pl.BlockSpec((D_MODEL, D_MODEL), lambda i: (0, 0))
    return pl.pallas_call(
        _merge_kernel,
        grid=(n // TM_OUT,),
        in_specs=[row(0), row(0), row(0), row(0), row(1), w_spec, w_spec, w_spec],
        out_specs=row(0),
        out_shape=jax.ShapeDtypeStruct((n, D_MODEL), F32),
        compiler_params=_params("parallel"),
        name="merge_out",
    )(x, hc, ao, gates, gates, wc, wn, wo)


def _ple_kernel(x_ref, p_ref, g_ref, gp_ref, wg_ref, wp_ref, o_ref):
    x = x_ref[...]
    e = _rms(_dot(p_ref[...].astype(wp_ref.dtype), wp_ref[...]), gp_ref[...])
    gate = jax.nn.sigmoid(_dot(_rms(x, g_ref[...]).astype(wg_ref.dtype), wg_ref[...]))
    o_ref[...] = x + gate * e


def _ple(x, p, g, g_post, w_gate, w_proj):
    n = x.shape[0]
    vec = pl.BlockSpec((1, D_MODEL), lambda i: (0, 0))
    return pl.pallas_call(
        _ple_kernel,
        grid=(n // TM_OUT,),
        in_specs=[
            pl.BlockSpec((TM_OUT, D_MODEL), lambda i: (i, 0)),
            pl.BlockSpec((TM_OUT, PLE_DIM), lambda i: (i, 0)),
            vec, vec,
            pl.BlockSpec((D_MODEL, D_MODEL), lambda i: (0, 0)),
            pl.BlockSpec((PLE_DIM, D_MODEL), lambda i: (0, 0)),
        ],
        out_specs=pl.BlockSpec((TM_OUT, D_MODEL), lambda i: (i, 0)),
        out_shape=jax.ShapeDtypeStruct((n, D_MODEL), F32),
        compiler_params=_params("parallel"),
        name="ple",
    )(x, p, g, g_post, w_gate, w_proj)


def _rope_tables(pos, width):
    half = ROPE_DIM // 2
    inv = ROPE_THETA ** (-jnp.arange(half, dtype=F32) / half)
    ang = pos.astype(F32)[:, None] * inv[None, :]
    cos, sin = jnp.cos(ang), jnp.sin(ang)
    n = pos.shape[0]
    rest = HEAD_DIM - ROPE_DIM
    seg_cos = jnp.concatenate([cos, cos, jnp.ones((n, rest), F32)], axis=1)
    seg_lo = jnp.concatenate([-sin, jnp.zeros((n, HEAD_DIM - half), F32)], axis=1)
    seg_hi = jnp.concatenate([jnp.zeros((n, half), F32), sin, jnp.zeros((n, rest), F32)], axis=1)
    reps = width // HEAD_DIM
    return tuple(jnp.tile(t, (1, reps)) for t in (seg_cos, seg_lo, seg_hi))


def kernel(x, p, ffn1_norm, ffn1_w_in, ffn1_w_out, mix_norm, w_in, b_in, conv_w, conv_b, conv_ln_g, conv_ln_b, conv_w_out, q_norm, k_norm, cmp_pos, cmp_w1, cmp_b1, cmp_w2, nsa_w_out, w_out, ffn2_norm, ffn2_w_in, ffn2_w_out, ple_norm, ple_w_gate, ple_w_proj, ple_post_norm):
    bsz, seq, _ = x.shape
    depth = w_in.shape[0]
    n_rows = bsz * seq
    n_sub = seq // CMP_STRIDE
    n_blk = seq // SEL_LEN
    mx = MXU_DTYPE
    row = lambda v: v[:, None, :]

    def kv_cols(j):
        return slice(OFF_KV + j * KV_W, OFF_KV + (j + 1) * KV_W)

    def regroup(w):
        glu = w[..., 0:OFF_Q]
        qk = jnp.concatenate([w[..., OFF_Q:OFF_KV], w[..., kv_cols(2)], w[..., kv_cols(4)]], axis=-1)
        plain = jnp.concatenate([w[..., kv_cols(0)], w[..., kv_cols(1)], w[..., kv_cols(3)], w[..., kv_cols(5)]], axis=-1)
        pad = jnp.zeros(w.shape[:-1] + (TN - (OFF_MERGE - OFF_GATE),), w.dtype)
        sig = jnp.concatenate([w[..., OFF_MERGE:N_IN], w[..., OFF_GATE:OFF_MERGE], pad], axis=-1)
        return glu, qk, plain, sig

    w_glu, w_qk, w_plain, w_sig = (w.astype(mx) for w in regroup(w_in))
    b_glu, b_qk, b_plain, b_sig = (row(b) for b in regroup(b_in))

    q_gain = jnp.tile(q_norm, (1, HPG)) * (HEAD_DIM ** -0.5 * LOG2_E)
    head_gain = jnp.concatenate(
        [jnp.tile(q_gain, (1, N_HEADS // HPG)), jnp.tile(k_norm[:, 1], (1, N_KV)), jnp.tile(k_norm[:, 2], (1, N_KV))],
        axis=-1)[:, None, :]
    cmp_gain = jnp.concatenate([k_norm[:, 0], jnp.zeros((depth, LANES - HEAD_DIM), F32)], axis=-1)[:, None, :]
    w2_pad = jnp.concatenate([cmp_w2, jnp.zeros(cmp_w2.shape[:-1] + (LANES - HEAD_DIM,), F32)], axis=-1).astype(mx)
    sub_w = CMP_STRIDE * HEAD_DIM
    pos_flat = cmp_pos.reshape(depth, 2, 2, 1, sub_w)

    layers = dict(
        ffn1_norm=row(ffn1_norm), ffn1_w_in=ffn1_w_in.astype(mx), ffn1_w_out=ffn1_w_out.astype(mx),
        mix_norm=row(mix_norm), w_glu=w_glu, w_qk=w_qk, w_plain=w_plain, w_sig=w_sig,
        b_glu=b_glu, b_qk=b_qk, b_plain=b_plain, b_sig=b_sig, head_gain=head_gain,
        conv_w=conv_w, conv_b=row(conv_b), conv_ln_g=row(conv_ln_g), conv_ln_b=row(conv_ln_b),
        conv_w_out=conv_w_out.astype(mx), cmp_gain=cmp_gain, pos_flat=pos_flat,
        cmp_w1=cmp_w1.astype(mx), cmp_b1=cmp_b1[:, :, None, :], cmp_w2=w2_pad,
        nsa_w_out=nsa_w_out.astype(mx), w_out=w_out.astype(mx),
        ffn2_norm=row(ffn2_norm), ffn2_w_in=ffn2_w_in.astype(mx), ffn2_w_out=ffn2_w_out.astype(mx),
        ple_norm=row(ple_norm), ple_w_gate=ple_w_gate.astype(mx), ple_w_proj=ple_w_proj.astype(mx),
        ple_post_norm=row(ple_post_norm), p=p.reshape(depth, n_rows, PLE_DIM),
    )

    tok_cos, tok_lo, tok_hi = _rope_tables(jnp.arange(seq), TN)
    cmp_cos, cmp_lo, cmp_hi = _rope_tables(jnp.arange(n_sub) * CMP_STRIDE + CMP_LEN - 1, LANES)
    seg = jnp.arange(TN) // HEAD_DIM
    ones_bd = (seg[:, None] == seg[None, :]).astype(mx)
    ci = jnp.arange(n_sub)[None, :]
    sj = jnp.arange(n_blk)[:, None]
    c2s = ((ci * CMP_STRIDE < (sj + 1) * SEL_LEN) & (ci * CMP_STRIDE + CMP_LEN > sj * SEL_LEN)).astype(mx)
    blk_of_key = ((jnp.arange(seq) // SEL_LEN)[:, None] == jnp.arange(LANES)[None, :]).astype(mx)
    k_aug_tail = jnp.concatenate([jnp.zeros((seq, LANES - HEAD_DIM), mx), blk_of_key], axis=1)
    k_aug_tail = jnp.broadcast_to(k_aug_tail, (bsz, N_KV) + k_aug_tail.shape)
    v_aug_tail = jnp.concatenate([jnp.ones((1, seq), mx), jnp.zeros((V_ROWS - HEAD_DIM - 1, seq), mx)], axis=0)
    v_aug_tail = jnp.broadcast_to(v_aug_tail, (bsz, N_KV) + v_aug_tail.shape)

    def by_group(a):
        return a.reshape(bsz, seq, N_KV, HEAD_DIM).transpose(0, 2, 1, 3)

    def by_group_t(a):
        return a.reshape(bsz, seq, N_KV, HEAD_DIM).transpose(0, 2, 3, 1)

    def sub_blocks(a):
        a = a.reshape(bsz, n_sub, CMP_STRIDE, N_KV, HEAD_DIM).transpose(0, 3, 1, 2, 4)
        return a.reshape(bsz, N_KV, n_sub, sub_w)

    def layer(xf, lp):
        xf = _ffn(xf, lp["ffn1_norm"], lp["ffn1_w_in"], lp["ffn1_w_out"])

        h_glu = _glu_proj(xf, lp["mix_norm"], lp["w_glu"], lp["b_glu"])
        qk = _qk_proj(xf, lp["mix_norm"], lp["w_qk"], lp["b_qk"], lp["head_gain"], ones_bd,
                      tok_cos, tok_lo, tok_hi, seq)
        plain = _act_proj(xf, lp["mix_norm"], lp["w_plain"], lp["b_plain"], sigmoid=False, name="kv_proj")
        sig = _act_proj(xf, lp["mix_norm"], lp["w_sig"], lp["b_sig"], sigmoid=True, name="gate_proj")

        hc = _conv_module(h_glu.reshape(bsz, seq, CONV_DIM), lp["conv_w"], lp["conv_b"],
                          lp["conv_ln_g"], lp["conv_ln_b"], mx).reshape(n_rows, CONV_DIM)

        comp = []
        for j, is_key in ((0, True), (1, False)):
            pos = lp["pos_flat"][j]
            comp.append(_compress(
                sub_blocks(plain[:, j * KV_W:(j + 1) * KV_W]), pos[0], pos[1],
                lp["cmp_w1"][j, :sub_w], lp["cmp_w1"][j, sub_w:], lp["cmp_b1"][j], lp["cmp_w2"][j],
                lp["cmp_gain"], cmp_cos, cmp_lo, cmp_hi, is_key=is_key, out_dtype=mx))
        kc = comp[0]
        vct = comp[1].transpose(0, 1, 3, 2)

        qt = qk[:, :N_HEADS * HEAD_DIM].reshape(bsz, seq, N_KV, HPG, HEAD_DIM).transpose(0, 2, 3, 4, 1)
        off = N_HEADS * HEAD_DIM
        ks = jnp.concatenate([by_group(qk[:, off:off + KV_W]), k_aug_tail], axis=3)
        kw = by_group(qk[:, off + KV_W:off + 2 * KV_W])
        vst = jnp.concatenate([by_group_t(plain[:, 2 * KV_W:3 * KV_W]).astype(mx), v_aug_tail], axis=2)
        vwt = jnp.concatenate([by_group_t(plain[:, 3 * KV_W:4 * KV_W]).astype(mx), v_aug_tail], axis=2)
        gates = sig[:, 2 * D_MODEL:2 * D_MODEL + 3 * N_HEADS]
        gates = gates.reshape(bsz, seq, N_KV, HPG, 3).transpose(0, 2, 4, 3, 1)
        ao = _nsa(qt, ks, vst, kw, vwt, kc, vct, c2s, gates)
        ao = ao.transpose(0, 4, 1, 2, 3).reshape(n_rows, N_HEADS * HEAD_DIM)

        xf = _merge(xf, hc, ao, sig, lp["conv_w_out"], lp["nsa_w_out"], lp["w_out"])
        xf = _ffn(xf, lp["ffn2_norm"], lp["ffn2_w_in"], lp["ffn2_w_out"])
        xf = _ple(xf, lp["p"], lp["ple_norm"], lp["ple_post_norm"], lp["ple_w_gate"], lp["ple_w_proj"])
        return xf, None

    out, _ = lax.scan(layer, x.reshape(n_rows, D_MODEL), layers)
    return out.reshape(bsz, seq, D_MODEL)
```

```python
import functools

import jax
import jax.numpy as jnp
from jax import lax
from jax.experimental import pallas as pl
from jax.experimental.pallas import tpu as pltpu

D_MODEL = 1024
PLE_DIM = 256
D_FF = 2816
CONV_DIM = D_MODEL
CONV_K = 31
N_HEADS = 16
N_KV = 4
HPG = N_HEADS // N_KV
HEAD_DIM = 64
ROPE_DIM = HEAD_DIM // 4
ROPE_THETA = 500000.0
CMP_LEN = 32
CMP_STRIDE = 16
CMP_HID = 256
SEL_LEN = 64
N_SEL = 16
WINDOW = 512
NEG = -1e30
EPS = 1e-6
KV_W = N_KV * HEAD_DIM
OFF_Q = 2 * CONV_DIM
OFF_KV = OFF_Q + N_HEADS * HEAD_DIM
OFF_GATE = OFF_KV + 6 * KV_W
OFF_MERGE = OFF_GATE + 3 * N_HEADS
N_IN = OFF_MERGE + 2 * D_MODEL

MXU_DTYPE = jnp.bfloat16
F32 = jnp.float32
LANES = 128
SUBLANES = 8
VMEM_LIMIT = 56 * 1024 * 1024

TM = 1024
TN = 256
TF = 256
TM_OUT = 512
TT = 256
HALO = 32
CONV_ROWS = 32
CONV_COLS = 256
TQ = 128
TK_SEL = 256
SEL_GROUP = 4
SEL_AHEAD = 2
MASK_BIG = 2.0 ** 100
LOG2_E = 1.4426950408889634
V_ROWS = HEAD_DIM + 16

TM_PROJ = 512
K_PAD_W = N_KV * LANES
RM_GLU_A, RM_GLU_G, RM_MERGE = 0, CONV_DIM, 2 * CONV_DIM
RM_KS = RM_MERGE + 2 * D_MODEL
RM_KW = RM_KS + K_PAD_W
RM_KVC = RM_KW + K_PAD_W
RM_END = RM_KVC + 2 * KV_W
TR_VS = N_HEADS * HEAD_DIM
TR_VW = TR_VS + N_KV * V_ROWS
TR_NG = TR_VW + N_KV * V_ROWS
TR_END = TR_NG + 64


def _params(*sem):
    return pltpu.CompilerParams(dimension_semantics=sem, vmem_limit_bytes=VMEM_LIMIT)


def _rms(x, g):
    return x * lax.rsqrt(jnp.mean(x * x, axis=-1, keepdims=True) + EPS) * g


def _dot(a, b):
    return jnp.dot(a, b, preferred_element_type=F32)


def _col_reduce(reduce_fn, combine_fn, s):
    rows = [s[r:r + SUBLANES] for r in range(0, s.shape[0], SUBLANES)]
    while len(rows) > 1:
        paired = [combine_fn(rows[i], rows[i + 1]) for i in range(0, len(rows) - 1, 2)]
        rows = paired + rows[len(rows) - len(rows) % 2:]
    return reduce_fn(rows[0], axis=0, keepdims=True)


def _ffn_kernel(x_ref, g_ref, wa_ref, wb_ref, wo_ref, o_ref, h_scr, acc_scr):
    f = pl.program_id(1)

    @pl.when(f == 0)
    def _():
        h_scr[...] = _rms(x_ref[...], g_ref[...]).astype(h_scr.dtype)
        acc_scr[...] = jnp.zeros_like(acc_scr)

    h = h_scr[...]
    a = _dot(h, wa_ref[...])
    b = _dot(h, wb_ref[...])
    act = (a * jax.nn.sigmoid(a) * b).astype(wo_ref.dtype)
    acc_scr[...] += _dot(act, wo_ref[...])

    @pl.when(f == pl.num_programs(1) - 1)
    def _():
        o_ref[...] = x_ref[...] + 0.5 * acc_scr[...]


def _ffn(x, g, w_in, w_out):
    n = x.shape[0]
    nf = D_FF // TF
    return pl.pallas_call(
        _ffn_kernel,
        grid=(n // TM, nf),
        in_specs=[
            pl.BlockSpec((TM, D_MODEL), lambda i, f: (i, 0)),
            pl.BlockSpec((1, D_MODEL), lambda i, f: (0, 0)),
            pl.BlockSpec((D_MODEL, TF), lambda i, f: (0, f)),
            pl.BlockSpec((D_MODEL, TF), lambda i, f: (0, f + nf)),
            pl.BlockSpec((TF, D_MODEL), lambda i, f: (f, 0)),
        ],
        out_specs=pl.BlockSpec((TM, D_MODEL), lambda i, f: (i, 0)),
        out_shape=jax.ShapeDtypeStruct((n, D_MODEL), F32),
        scratch_shapes=[pltpu.VMEM((TM, D_MODEL), w_in.dtype), pltpu.VMEM((TM, D_MODEL), F32)],
        compiler_params=_params("parallel", "arbitrary"),
        name="ffn",
    )(x, g, w_in, w_in, w_out)


def _head_norm_rope(y, ones_bd, gain, cos, sin_lo, sin_hi):
    ss = _dot((y * y).astype(ones_bd.dtype), ones_bd)
    yn = y * lax.rsqrt(ss * (1.0 / HEAD_DIM) + EPS) * gain
    width = y.shape[-1]
    half = ROPE_DIM // 2
    return (yn * cos + pltpu.roll(yn, width - half, 1) * sin_lo + pltpu.roll(yn, half, 1) * sin_hi)


def _mix_proj_kernel(x_ref, g_ref, wr_ref, br_ref, wt_ref, bt_ref, kg_ref, qg_ref, bd_ref,
                     cos_ref, slo_ref, shi_ref, cost_ref, sint_ref,
                     hglu_ref, mg_ref, ks_ref, kw_ref, kvc_ref, qt_ref, vst_ref, vwt_ref, ngt_ref):
    h = _rms(x_ref[...], g_ref[...]).astype(wr_ref.dtype)
    tm = h.shape[0]

    def rows(lo, hi):
        return _dot(h, wr_ref[:, lo:hi]) + br_ref[:, lo:hi]

    hglu_ref[...] = rows(RM_GLU_A, RM_GLU_G) * jax.nn.sigmoid(rows(RM_GLU_G, RM_MERGE))
    mg_ref[...] = jax.nn.sigmoid(rows(RM_MERGE, RM_KS))
    for out_ref, base in ((ks_ref, RM_KS), (kw_ref, RM_KW)):
        for c0 in range(0, K_PAD_W, TN):
            y = rows(base + c0, base + c0 + TN)
            gain = kg_ref[:, base - RM_KS + c0:base - RM_KS + c0 + TN]
            out = _head_norm_rope(y, bd_ref[...], gain, cos_ref[...], slo_ref[...], shi_ref[...])
            out_ref[:, c0:c0 + TN] = out.astype(out_ref.dtype)
    kvc_ref[...] = rows(RM_KVC, RM_END)

    lanes = lambda ref, lo, hi: jnp.tile(ref[lo:hi, :], (1, tm // LANES))
    yt = lax.dot_general(wt_ref[...], h, (((1,), (1,)), ((), ())), preferred_element_type=F32)

    half = ROPE_DIM // 2
    cos_t, sin_t = cost_ref[...], sint_ref[...]
    for hd in range(N_HEADS):
        r0 = hd * HEAD_DIM
        y = yt[r0:r0 + HEAD_DIM] + lanes(bt_ref, r0, r0 + HEAD_DIM)
        ss = jnp.sum(y * y, axis=0, keepdims=True)
        yn = y * lax.rsqrt(ss * (1.0 / HEAD_DIM) + EPS) * lanes(qg_ref, r0, r0 + HEAD_DIM)
        x1, x2 = yn[0:half], yn[half:ROPE_DIM]
        rot = jnp.concatenate([x1 * cos_t - x2 * sin_t, x2 * cos_t + x1 * sin_t, yn[ROPE_DIM:]], axis=0)
        qt_ref[r0:r0 + HEAD_DIM, :] = rot.astype(qt_ref.dtype)

    vst_ref[...] = (yt[TR_VS:TR_VW] + lanes(bt_ref, TR_VS, TR_VW)).astype(vst_ref.dtype)
    vwt_ref[...] = (yt[TR_VW:TR_NG] + lanes(bt_ref, TR_VW, TR_NG)).astype(vwt_ref.dtype)
    n_gate = ngt_ref.shape[0]
    ngt_ref[...] = jax.nn.sigmoid(yt[TR_NG:TR_NG + n_gate] + lanes(bt_ref, TR_NG, TR_NG + n_gate))


def _mix_proj(x, g, w_rows, b_rows, w_t, b_t, k_gain, q_gain_t, ones_bd, cos, sin_lo, sin_hi, cos_t, sin_t, bsz, seq):
    n = x.shape[0]
    t_tiles = seq // TM_PROJ
    const = lambda shape: pl.BlockSpec(shape, lambda i: (0,) * len(shape), pipeline_mode=pl.Buffered(1))
    by_row = lambda width: pl.BlockSpec((TM_PROJ, width), lambda i: (i, 0))
    by_time = lambda n_rows: pl.BlockSpec((None, n_rows, TM_PROJ), lambda i: (i // t_tiles, 0, i % t_tiles))
    mx = w_rows.dtype
    n_gate = 3 * N_HEADS
    return pl.pallas_call(
        _mix_proj_kernel,
        grid=(n // TM_PROJ,),
        in_specs=[
            by_row(D_MODEL), const((1, D_MODEL)),
            const(w_rows.shape), const(b_rows.shape), const(w_t.shape), const(b_t.shape),
            const(k_gain.shape), const(q_gain_t.shape), const(ones_bd.shape),
            pl.BlockSpec((TM_PROJ, TN), lambda i: (i % t_tiles, 0)),
            pl.BlockSpec((TM_PROJ, TN), lambda i: (i % t_tiles, 0)),
            pl.BlockSpec((TM_PROJ, TN), lambda i: (i % t_tiles, 0)),
            pl.BlockSpec((ROPE_DIM // 2, TM_PROJ), lambda i: (0, i % t_tiles)),
            pl.BlockSpec((ROPE_DIM // 2, TM_PROJ), lambda i: (0, i % t_tiles)),
        ],
        out_specs=[
            by_row(CONV_DIM), by_row(2 * D_MODEL), by_row(K_PAD_W), by_row(K_PAD_W), by_row(2 * KV_W),
            by_time(N_HEADS * HEAD_DIM), by_time(N_KV * V_ROWS), by_time(N_KV * V_ROWS), by_time(n_gate),
        ],
        out_shape=[
            jax.ShapeDtypeStruct((n, CONV_DIM), F32), jax.ShapeDtypeStruct((n, 2 * D_MODEL), F32),
            jax.ShapeDtypeStruct((n, K_PAD_W), mx), jax.ShapeDtypeStruct((n, K_PAD_W), mx),
            jax.ShapeDtypeStruct((n, 2 * KV_W), F32),
            jax.ShapeDtypeStruct((bsz, N_HEADS * HEAD_DIM, seq), mx),
            jax.ShapeDtypeStruct((bsz, N_KV * V_ROWS, seq), mx), jax.ShapeDtypeStruct((bsz, N_KV * V_ROWS, seq), mx),
            jax.ShapeDtypeStruct((bsz, n_gate, seq), F32),
        ],
        compiler_params=_params("parallel"),
        name="mix_proj",
    )(x, g, w_rows, b_rows, w_t, b_t, k_gain, q_gain_t, ones_bd, cos, sin_lo, sin_hi, cos_t, sin_t)


def _conv_kernel(cur_ref, halo_ref, w_ref, b_ref, lg_ref, lb_ref, o_ref, sh_scr, conv_scr):
    t = pl.program_id(1)
    ext = TT + HALO - 8
    sh_scr[0, 0:HALO, :] = jnp.where(t == 0, 0.0, halo_ref[...])
    sh_scr[0, HALO:HALO + TT, :] = cur_ref[...]
    for r in range(1, 8):
        sh_scr[r, 0:ext, :] = sh_scr[0, r:r + ext, :]

    lead = HALO - (CONV_K - 1)

    def row_chunk(ri, carry):
        r0 = pl.multiple_of(ri * CONV_ROWS, CONV_ROWS)
        for c0 in range(0, CONV_DIM, CONV_COLS):
            acc = jnp.broadcast_to(b_ref[:, c0:c0 + CONV_COLS], (CONV_ROWS, CONV_COLS))
            for k in range(CONV_K):
                off = k + lead
                base = pl.multiple_of(r0 + (off // 8) * 8, 8)
                acc = acc + w_ref[k:k + 1, c0:c0 + CONV_COLS] * sh_scr[off % 8, pl.ds(base, CONV_ROWS),
                                                                       c0:c0 + CONV_COLS]
            conv_scr[pl.ds(r0, CONV_ROWS), c0:c0 + CONV_COLS] = acc
        return carry

    lax.fori_loop(0, TT // CONV_ROWS, row_chunk, 0)

    h = conv_scr[...]
    mu = jnp.mean(h, axis=-1, keepdims=True)
    var = jnp.mean(jnp.square(h - mu), axis=-1, keepdims=True)
    y = (h - mu) * lax.rsqrt(var + EPS) * lg_ref[...] + lb_ref[...]
    o_ref[...] = (y * jax.nn.sigmoid(y)).astype(o_ref.dtype)


def _conv_module(h, w, b, ln_g, ln_b, out_dtype):
    bsz, seq, _ = h.shape
    halo_per_tile = TT // HALO
    vec = pl.BlockSpec((1, CONV_DIM), lambda bi, t: (0, 0))
    return pl.pallas_call(
        _conv_kernel,
        grid=(bsz, seq // TT),
        in_specs=[
            pl.BlockSpec((None, TT, CONV_DIM), lambda bi, t: (bi, t, 0)),
            pl.BlockSpec((None, HALO, CONV_DIM), lambda bi, t: (bi, jnp.maximum(t * halo_per_tile - 1, 0), 0)),
            pl.BlockSpec((CONV_K, CONV_DIM), lambda bi, t: (0, 0)),
            vec, vec, vec,
        ],
        out_specs=pl.BlockSpec((None, TT, CONV_DIM), lambda bi, t: (bi, t, 0)),
        out_shape=jax.ShapeDtypeStruct((bsz, seq, CONV_DIM), out_dtype),
        scratch_shapes=[pltpu.VMEM((8, TT + HALO, CONV_DIM), F32), pltpu.VMEM((TT, CONV_DIM), F32)],
        compiler_params=_params("parallel", "parallel"),
        name="conv_module",
    )(h, h, w, b, ln_g, ln_b)


def _compress_kernel(a_ref, plo_ref, phi_ref, w1a_ref, w1b_ref, b1_ref, w2_ref,
                     hg_ref, cos_ref, slo_ref, shi_ref, o_ref, *, is_key):
    a = a_ref[...]
    n = a.shape[0]
    p = _dot((a + plo_ref[...]).astype(w1a_ref.dtype), w1a_ref[...])
    q = _dot((a + phi_ref[...]).astype(w1b_ref.dtype), w1b_ref[...])
    hid = p + pltpu.roll(q, n - 1, 0) + b1_ref[...]
    hid = jax.nn.gelu(hid, approximate=True)
    c = _dot(hid.astype(w2_ref.dtype), w2_ref[...])
    if is_key:
        ss = jnp.sum(c * c, axis=-1, keepdims=True)
        cn = c * lax.rsqrt(ss * (1.0 / HEAD_DIM) + EPS) * hg_ref[...]
        half = ROPE_DIM // 2
        c = cn * cos_ref[...] + pltpu.roll(cn, LANES - half, 1) * slo_ref[...] + pltpu.roll(cn, half, 1) * shi_ref[...]
    o_ref[...] = c[:, :HEAD_DIM].astype(o_ref.dtype)


def _compress(a, pos_lo, pos_hi, w1a, w1b, b1, w2, head_gain, cos, sin_lo, sin_hi, *, is_key, out_dtype):
    bsz, ng, n_sub, width = a.shape
    full = lambda *shape: pl.BlockSpec(shape, lambda bi, gi: (0,) * len(shape))
    return pl.pallas_call(
        functools.partial(_compress_kernel, is_key=is_key),
        grid=(bsz, ng),
        in_specs=[
            pl.BlockSpec((None, None, n_sub, width), lambda bi, gi: (bi, gi, 0, 0)),
            full(1, width), full(1, width), full(width, CMP_HID), full(width, CMP_HID),
            full(1, CMP_HID), full(CMP_HID, LANES), full(1, LANES),
            full(n_sub, LANES), full(n_sub, LANES), full(n_sub, LANES),
        ],
        out_specs=pl.BlockSpec((None, None, n_sub, HEAD_DIM), lambda bi, gi: (bi, gi, 0, 0)),
        out_shape=jax.ShapeDtypeStruct((bsz, ng, n_sub, HEAD_DIM), out_dtype),
        compiler_params=_params("parallel", "parallel"),
        name="compress_k" if is_key else "compress_v",
    )(a, pos_lo, pos_hi, w1a, w1b, b1, w2, head_gain, cos, sin_lo, sin_hi)


def _nsa_kernel(qt_ref, ks_ref, blk_ref, vst_ref, kw_ref, vwt_ref, kc_ref, vct_ref, c2s_ref, g_ref, o_ref,
                qaug_scr, s_scr):
    m_cols = HPG * TQ
    t0 = pl.program_id(2) * TQ
    qt = jnp.concatenate([qt_ref[h] for h in range(HPG)], axis=1)
    qt_pad = jnp.concatenate([qt, jnp.zeros((LANES - HEAD_DIM, m_cols), qt.dtype)], axis=0)
    col_t = t0 + (lax.broadcasted_iota(jnp.int32, (1, m_cols), 1) & (TQ - 1))

    n_cmp = kc_ref.shape[0]
    s = _dot(kc_ref[...], qt)
    cmp_end = lax.broadcasted_iota(jnp.int32, (n_cmp, 1), 0) * CMP_STRIDE + (CMP_LEN - 1)
    cmask = cmp_end <= col_t
    s = jnp.where(cmask, s, NEG)
    p = jnp.exp2(s - jnp.maximum(_col_reduce(jnp.max, jnp.maximum, s), 0.1 * NEG))
    l = _col_reduce(jnp.sum, jnp.add, p)
    pc = p * (1.0 / jnp.where(l > 0.0, l, 1.0))
    o_c = _dot(vct_ref[...], pc.astype(vct_ref.dtype))

    tq_io = lax.broadcasted_iota(jnp.int32, (TQ, 1), 0)
    w0 = pl.multiple_of(jnp.maximum(t0 - WINDOW, 0), TQ)
    s = _dot(kw_ref[pl.ds(w0, WINDOW + TQ), :], qt_pad)
    s = jnp.concatenate([jnp.where(w0 + tq_io > col_t - WINDOW, s[0:TQ], -MASK_BIG), s[TQ:]], axis=0)
    s = jnp.where(w0 + lax.broadcasted_iota(jnp.int32, (WINDOW + TQ, 1), 0) <= col_t, s, -MASK_BIG)
    p = jnp.exp2(s - _col_reduce(jnp.max, jnp.maximum, s))
    acc_w = _dot(vwt_ref[:, pl.ds(w0, WINDOW + TQ)], p.astype(vwt_ref.dtype))
    acc_w, l_w = acc_w[0:HEAD_DIM], acc_w[HEAD_DIM:HEAD_DIM + 1]

    pc_sum = pc[:, 0:TQ] + pc[:, TQ:2 * TQ] + pc[:, 2 * TQ:3 * TQ] + pc[:, 3 * TQ:4 * TQ]
    c2s = c2s_ref[...]
    hi = pc_sum.astype(c2s.dtype)
    lo = (pc_sum - hi.astype(F32)).astype(c2s.dtype)
    imp = _dot(c2s, hi) + _dot(c2s, lo)
    n_blk = imp.shape[0]
    j_io = lax.broadcasted_iota(jnp.int32, (n_blk, TQ), 0)
    t_io = t0 + lax.broadcasted_iota(jnp.int32, (n_blk, TQ), 1)
    cur = t_io // SEL_LEN
    forced = (j_io == 0) | (j_io == cur) | (j_io == cur - 1)
    valid = j_io * SEL_LEN <= t_io
    imp = jnp.where(valid, jnp.where(forced, 1e4, imp), -1.0)

    sel = jnp.zeros((n_blk, TQ), F32)
    j_f = j_io.astype(F32)
    for _ in range(min(N_SEL, n_blk)):
        mx = jnp.max(imp, axis=0, keepdims=True)
        first = jnp.min(jnp.where(imp == mx, j_f, float(n_blk)), axis=0, keepdims=True)
        pick = j_f == first
        sel = jnp.where(pick & (mx >= 0.0), 1.0, sel)
        imp = jnp.where(pick, -3e38, imp)
    bias = ((jnp.where(j_io == cur, 0.0, sel) - 1.0) * MASK_BIG).astype(qaug_scr.dtype)

    qaug_scr[0:LANES, :] = qt_pad
    if n_blk < LANES:
        qaug_scr[LANES + n_blk:, :] = jnp.zeros((LANES - n_blk, m_cols), qaug_scr.dtype)
    for h in range(HPG):
        qaug_scr[LANES:LANES + n_blk, h * TQ:(h + 1) * TQ] = bias

    def online(s, vt, carry):
        m_prev, acc = carry
        m_new = jnp.maximum(m_prev, _col_reduce(jnp.max, jnp.maximum, s))
        alpha = jnp.exp2(m_prev - m_new)
        p = jnp.exp2(s - m_new)
        return m_new, alpha * acc + _dot(vt, p.astype(vt.dtype))

    def finish(carry):
        _, acc = carry
        return acc[0:HEAD_DIM], acc[HEAD_DIM:HEAD_DIM + 1]

    init = (jnp.full((1, m_cols), -jnp.inf, F32), jnp.zeros((V_ROWS, m_cols), F32))

    t0a = pl.multiple_of(t0, TQ)
    own = ((t0 + tq_io) // SEL_LEN == col_t // SEL_LEN) & (t0 + tq_io <= col_t)
    s = jnp.where(own, _dot(ks_ref[pl.ds(t0a, TQ), :], qt_pad), -MASK_BIG)
    carry = online(s, vst_ref[:, pl.ds(t0a, TQ)], init)

    def sel_scores(tile):
        rows = pl.ds(pl.multiple_of(tile * TK_SEL, TK_SEL), TK_SEL)
        return _dot(jnp.concatenate([ks_ref[rows, :], blk_ref[rows, :]], axis=1), qaug_scr[...])

    last_tile = ks_ref.shape[0] // TK_SEL - 1
    for ahead in range(SEL_AHEAD):
        s_scr[ahead] = sel_scores(ahead)

    def sel_group(i, carry):
        first = SEL_GROUP * i
        scores = [s_scr[ahead] for ahead in range(SEL_AHEAD)]
        scores += [sel_scores(jnp.minimum(first + SEL_AHEAD + ahead, last_tile)) for ahead in range(SEL_AHEAD)]
        for part in range(SEL_GROUP):
            k0 = pl.multiple_of((first + part) * TK_SEL, TK_SEL)
            carry = online(scores[part], vst_ref[:, pl.ds(k0, TK_SEL)], carry)
            if part + 2 * SEL_AHEAD < SEL_GROUP + SEL_AHEAD:
                scores.append(sel_scores(jnp.minimum(first + part + 2 * SEL_AHEAD, last_tile)))
        for ahead in range(SEL_AHEAD):
            s_scr[ahead] = scores[SEL_GROUP + ahead]
        return carry

    acc_s, l_s = finish(lax.fori_loop(0, t0 // (SEL_GROUP * TK_SEL) + 1, sel_group, carry))

    gate = lambda c: jnp.concatenate([g_ref[c, h:h + 1, :] for h in range(HPG)], axis=1)
    out = gate(0) * o_c + (gate(1) * (1.0 / l_s)) * acc_s + (gate(2) * (1.0 / l_w)) * acc_w
    for h in range(HPG):
        o_ref[h] = out[:, h * TQ:(h + 1) * TQ].astype(o_ref.dtype)


def _nsa(qt, ks, blk, vst, kw, vwt, kc, vct, c2s, gates):
    bsz, ng, _, _, seq = qt.shape
    n_cmp = kc.shape[2]
    n_blk = c2s.shape[0]
    per_group = lambda *shape: pl.BlockSpec((None, None) + shape, lambda bi, gi, qi: (bi, gi) + (0,) * len(shape))
    q_tile = lambda *lead: pl.BlockSpec((None, None) + lead + (TQ,), lambda bi, gi, qi: (bi, gi) + (0,) * len(lead) + (qi,))
    key_slot = pl.BlockSpec((None, seq, LANES), lambda bi, gi, qi: (bi, 0, gi))
    return pl.pallas_call(
        _nsa_kernel,
        grid=(bsz, ng, seq // TQ),
        in_specs=[
            q_tile(HPG, HEAD_DIM),
            key_slot, pl.BlockSpec((seq, LANES), lambda bi, gi, qi: (0, 0)), per_group(V_ROWS, seq),
            key_slot, per_group(V_ROWS, seq),
            per_group(n_cmp, HEAD_DIM), per_group(HEAD_DIM, n_cmp),
            pl.BlockSpec((n_blk, n_cmp), lambda bi, gi, qi: (0, 0)),
            q_tile(3, HPG),
        ],
        out_specs=q_tile(HPG, HEAD_DIM),
        out_shape=jax.ShapeDtypeStruct((bsz, ng, HPG, HEAD_DIM, seq), qt.dtype),
        scratch_shapes=[pltpu.VMEM((2 * LANES, HPG * TQ), qt.dtype), pltpu.VMEM((SEL_AHEAD, TK_SEL, HPG * TQ), F32)],
        compiler_params=_params("parallel", "parallel", "arbitrary"),
        name="nsa",
    )(qt, ks, blk, vst, kw, vwt, kc, vct, c2s, gates)


def _merge_kernel(x_ref, hc_ref, aot_ref, gc_ref, gn_ref, wc_ref, wn_ref, wo_ref, o_ref):
    y_conv = _dot(hc_ref[...], wc_ref[...])
    y_nsa = lax.dot_general(aot_ref[...], wn_ref[...], (((0,), (0,)), ((), ())), preferred_element_type=F32)
    mixed = gc_ref[...] * y_conv + gn_ref[...] * y_nsa
    o_ref[...] = x_ref[...] + _dot(mixed.astype(wo_ref.dtype), wo_ref[...])


def _merge(x, hc, aot, gates, wc, wn, wo):
    n = x.shape[0]
    seq = aot.shape[2]
    t_tiles = seq // TM_OUT
    row = lambda j: pl.BlockSpec((TM_OUT, D_MODEL), lambda i: (i, j))
    w_spec = pl.BlockSpec((D_MODEL, D_MODEL), lambda i: (0, 0))
    aot_spec = pl.BlockSpec((None, D_MODEL, TM_OUT), lambda i: (i // t_tiles, 0, i % t_tiles))
    return pl.pallas_call(
        _merge_kernel,
        grid=(n // TM_OUT,),
        in_specs=[row(0), row(0), aot_spec, row(0), row(1), w_spec, w_spec, w_spec],
        out_specs=row(0),
        out_shape=jax.ShapeDtypeStruct((n, D_MODEL), F32),
        compiler_params=_params("parallel"),
        name="merge_out",
    )(x, hc, aot, gates, gates, wc, wn, wo)


def _ple_kernel(x_ref, p_ref, g_ref, gp_ref, wg_ref, wp_ref, o_ref):
    x = x_ref[...]
    e = _rms(_dot(p_ref[...].astype(wp_ref.dtype), wp_ref[...]), gp_ref[...])
    gate = jax.nn.sigmoid(_dot(_rms(x, g_ref[...]).astype(wg_ref.dtype), wg_ref[...]))
    o_ref[...] = x + gate * e


def _ple(x, p, g, g_post, w_gate, w_proj):
    n = x.shape[0]
    vec = pl.BlockSpec((1, D_MODEL), lambda i: (0, 0))
    return pl.pallas_call(
        _ple_kernel,
        grid=(n // TM_OUT,),
        in_specs=[
            pl.BlockSpec((TM_OUT, D_MODEL), lambda i: (i, 0)),
            pl.BlockSpec((TM_OUT, PLE_DIM), lambda i: (i, 0)),
            vec, vec,
            pl.BlockSpec((D_MODEL, D_MODEL), lambda i: (0, 0)),
            pl.BlockSpec((PLE_DIM, D_MODEL), lambda i: (0, 0)),
        ],
        out_specs=pl.BlockSpec((TM_OUT, D_MODEL), lambda i: (i, 0)),
        out_shape=jax.ShapeDtypeStruct((n, D_MODEL), F32),
        compiler_params=_params("parallel"),
        name="ple",
    )(x, p, g, g_post, w_gate, w_proj)


def _rope_tables(pos, width):
    half = ROPE_DIM // 2
    inv = ROPE_THETA ** (-jnp.arange(half, dtype=F32) / half)
    ang = pos.astype(F32)[:, None] * inv[None, :]
    cos, sin = jnp.cos(ang), jnp.sin(ang)
    n = pos.shape[0]
    rest = HEAD_DIM - ROPE_DIM
    seg_cos = jnp.concatenate([cos, cos, jnp.ones((n, rest), F32)], axis=1)
    seg_lo = jnp.concatenate([-sin, jnp.zeros((n, HEAD_DIM - half), F32)], axis=1)
    seg_hi = jnp.concatenate([jnp.zeros((n, half), F32), sin, jnp.zeros((n, rest), F32)], axis=1)
    reps = width // HEAD_DIM
    return tuple(jnp.tile(t, (1, reps)) for t in (seg_cos, seg_lo, seg_hi))


def kernel(x, p, ffn1_norm, ffn1_w_in, ffn1_w_out, mix_norm, w_in, b_in, conv_w, conv_b, conv_ln_g, conv_ln_b, conv_w_out, q_norm, k_norm, cmp_pos, cmp_w1, cmp_b1, cmp_w2, nsa_w_out, w_out, ffn2_norm, ffn2_w_in, ffn2_w_out, ple_norm, ple_w_gate, ple_w_proj, ple_post_norm):
    bsz, seq, _ = x.shape
    depth = w_in.shape[0]
    n_rows = bsz * seq
    n_sub = seq // CMP_STRIDE
    n_blk = seq // SEL_LEN
    mx = MXU_DTYPE
    row = lambda v: v[:, None, :]

    def kv_cols(j):
        return slice(OFF_KV + j * KV_W, OFF_KV + (j + 1) * KV_W)

    def group_slots(a, width, fill=None):
        a = a.reshape(a.shape[:-1] + (N_KV, HEAD_DIM))
        tail = jnp.zeros(a.shape[:-1] + (width - HEAD_DIM,), a.dtype)
        if fill is not None:
            tail = tail.at[..., 0].set(fill)
        a = jnp.concatenate([a, tail], axis=-1)
        return a.reshape(a.shape[:-2] + (N_KV * width,))

    def gate_order(a):
        a = a.reshape(a.shape[:-1] + (N_KV, HPG, 3))
        return jnp.swapaxes(a, -1, -2).reshape(a.shape[:-3] + (3 * N_HEADS,))

    def split(w, transposed_fill):
        rows_part = jnp.concatenate(
            [w[..., 0:OFF_Q], w[..., OFF_MERGE:N_IN], group_slots(w[..., kv_cols(2)], LANES),
             group_slots(w[..., kv_cols(4)], LANES), w[..., kv_cols(0)], w[..., kv_cols(1)]], axis=-1)
        gates_part = gate_order(w[..., OFF_GATE:OFF_MERGE])
        pad = jnp.zeros(w.shape[:-1] + (TR_END - TR_NG - 3 * N_HEADS,), w.dtype)
        t_part = jnp.concatenate(
            [w[..., OFF_Q:OFF_KV], group_slots(w[..., kv_cols(3)], V_ROWS, transposed_fill),
             group_slots(w[..., kv_cols(5)], V_ROWS, transposed_fill), gates_part, pad], axis=-1)
        return rows_part, t_part

    w_rows, w_t = split(w_in, None)
    w_rows, w_t = w_rows.astype(mx), jnp.swapaxes(w_t, 1, 2).astype(mx)
    b_rows, b_t = split(b_in, 1.0)
    lane_rep = lambda v: jnp.broadcast_to(v[:, :, None], v.shape + (LANES,))
    b_rows, b_t = row(b_rows), lane_rep(b_t)

    q_gain_t = lane_rep(jnp.tile(q_norm, (1, N_HEADS)) * (HEAD_DIM ** -0.5 * LOG2_E))
    k_gain = row(jnp.concatenate([group_slots(jnp.tile(k_norm[:, 1], (1, N_KV)), LANES),
                                  group_slots(jnp.tile(k_norm[:, 2], (1, N_KV)), LANES)], axis=-1))
    cmp_gain = jnp.concatenate([k_norm[:, 0], jnp.zeros((depth, LANES - HEAD_DIM), F32)], axis=-1)[:, None, :]
    w2_pad = jnp.concatenate([cmp_w2, jnp.zeros(cmp_w2.shape[:-1] + (LANES - HEAD_DIM,), F32)], axis=-1).astype(mx)
    sub_w = CMP_STRIDE * HEAD_DIM
    pos_flat = cmp_pos.reshape(depth, 2, 2, 1, sub_w)

    layers = dict(
        ffn1_norm=row(ffn1_norm), ffn1_w_in=ffn1_w_in.astype(mx), ffn1_w_out=ffn1_w_out.astype(mx),
        mix_norm=row(mix_norm), w_rows=w_rows, b_rows=b_rows, w_t=w_t, b_t=b_t, k_gain=k_gain, q_gain_t=q_gain_t,
        conv_w=conv_w, conv_b=row(conv_b), conv_ln_g=row(conv_ln_g), conv_ln_b=row(conv_ln_b),
        conv_w_out=conv_w_out.astype(mx), cmp_gain=cmp_gain, pos_flat=pos_flat,
        cmp_w1=cmp_w1.astype(mx), cmp_b1=cmp_b1[:, :, None, :], cmp_w2=w2_pad,
        nsa_w_out=nsa_w_out.astype(mx), w_out=w_out.astype(mx),
        ffn2_norm=row(ffn2_norm), ffn2_w_in=ffn2_w_in.astype(mx), ffn2_w_out=ffn2_w_out.astype(mx),
        ple_norm=row(ple_norm), ple_w_gate=ple_w_gate.astype(mx), ple_w_proj=ple_w_proj.astype(mx),
        ple_post_norm=row(ple_post_norm), p=p.reshape(depth, n_rows, PLE_DIM),
    )

    tok_cos, tok_lo, tok_hi = _rope_tables(jnp.arange(seq), TN)
    cmp_cos, cmp_lo, cmp_hi = _rope_tables(jnp.arange(n_sub) * CMP_STRIDE + CMP_LEN - 1, LANES)
    seg = jnp.arange(TN) // HEAD_DIM
    ones_bd = (seg[:, None] == seg[None, :]).astype(mx)
    ci = jnp.arange(n_sub)[None, :]
    sj = jnp.arange(n_blk)[:, None]
    c2s = ((ci * CMP_STRIDE < (sj + 1) * SEL_LEN) & (ci * CMP_STRIDE + CMP_LEN > sj * SEL_LEN)).astype(mx)
    blk_of_key = ((jnp.arange(seq) // SEL_LEN)[:, None] == jnp.arange(LANES)[None, :]).astype(mx)
    half = ROPE_DIM // 2
    ang_t = (ROPE_THETA ** (-jnp.arange(half, dtype=F32) / half))[:, None] * jnp.arange(seq).astype(F32)[None, :]
    cos_t, sin_t = jnp.cos(ang_t), jnp.sin(ang_t)

    def sub_blocks(a):
        a = a.reshape(bsz, n_sub, CMP_STRIDE, N_KV, HEAD_DIM).transpose(0, 3, 1, 2, 4)
        return a.reshape(bsz, N_KV, n_sub, sub_w)

    def layer(xf, lp):
        xf = _ffn(xf, lp["ffn1_norm"], lp["ffn1_w_in"], lp["ffn1_w_out"])

        h_glu, merge_gates, ks, kw, plain, qt, vst, vwt, gates = _mix_proj(
            xf, lp["mix_norm"], lp["w_rows"], lp["b_rows"], lp["w_t"], lp["b_t"], lp["k_gain"], lp["q_gain_t"],
            ones_bd, tok_cos, tok_lo, tok_hi, cos_t, sin_t, bsz, seq)

        hc = _conv_module(h_glu.reshape(bsz, seq, CONV_DIM), lp["conv_w"], lp["conv_b"],
                          lp["conv_ln_g"], lp["conv_ln_b"], mx).reshape(n_rows, CONV_DIM)

        comp = []
        for j, is_key in ((0, True), (1, False)):
            pos = lp["pos_flat"][j]
            comp.append(_compress(
                sub_blocks(plain[:, j * KV_W:(j + 1) * KV_W]), pos[0], pos[1],
                lp["cmp_w1"][j, :sub_w], lp["cmp_w1"][j, sub_w:], lp["cmp_b1"][j], lp["cmp_w2"][j],
                lp["cmp_gain"], cmp_cos, cmp_lo, cmp_hi, is_key=is_key, out_dtype=mx))
        kc = comp[0]
        vct = comp[1].transpose(0, 1, 3, 2)

        aot = _nsa(qt.reshape(bsz, N_KV, HPG, HEAD_DIM, seq), ks.reshape(bsz, seq, K_PAD_W), blk_of_key,
                   vst.reshape(bsz, N_KV, V_ROWS, seq), kw.reshape(bsz, seq, K_PAD_W),
                   vwt.reshape(bsz, N_KV, V_ROWS, seq), kc, vct, c2s,
                   gates.reshape(bsz, N_KV, 3, HPG, seq))
        aot = aot.reshape(bsz, N_HEADS * HEAD_DIM, seq)

        xf = _merge(xf, hc, aot, merge_gates, lp["conv_w_out"], lp["nsa_w_out"], lp["w_out"])
        xf = _ffn(xf, lp["ffn2_norm"], lp["ffn2_w_in"], lp["ffn2_w_out"])
        xf = _ple(xf, lp["p"], lp["ple_norm"], lp["ple_post_norm"], lp["ple_w_gate"], lp["ple_w_proj"])
        return xf, None

    out, _ = lax.scan(layer, x.reshape(n_rows, D_MODEL), layers)
    return out.reshape(bsz, seq, D_MODEL)
```

```python
import functools

import jax
import jax.numpy as jnp
from jax import lax
from jax.experimental import pallas as pl
from jax.experimental.pallas import tpu as pltpu

D_MODEL = 1024
PLE_DIM = 256
D_FF = 2816
CONV_DIM = D_MODEL
CONV_K = 31
N_HEADS = 16
N_KV = 4
HPG = N_HEADS // N_KV
HEAD_DIM = 64
ROPE_DIM = HEAD_DIM // 4
ROPE_THETA = 500000.0
CMP_LEN = 32
CMP_STRIDE = 16
CMP_HID = 256
SEL_LEN = 64
N_SEL = 16
WINDOW = 512
NEG = -1e30
EPS = 1e-6
KV_W = N_KV * HEAD_DIM
OFF_Q = 2 * CONV_DIM
OFF_KV = OFF_Q + N_HEADS * HEAD_DIM
OFF_GATE = OFF_KV + 6 * KV_W
OFF_MERGE = OFF_GATE + 3 * N_HEADS
N_IN = OFF_MERGE + 2 * D_MODEL

MXU_DTYPE = jnp.bfloat16
F32 = jnp.float32
LANES = 128
SUBLANES = 8
VMEM_LIMIT = 56 * 1024 * 1024

TM_FFN = 512
TN = 256
TF = 256
TM_OUT = 512
TT = 256
HALO = 32
CONV_ROWS = 32
CONV_COLS = 256
TQ = 128
TK_SEL = 256
SEL_GROUP = 4
SEL_AHEAD = 2
MASK_BIG = 2.0 ** 100
LOG2_E = 1.4426950408889634
V_ROWS = HEAD_DIM + 16

TM_PROJ = 512
K_PAD_W = N_KV * LANES
RM_GLU_A, RM_GLU_G, RM_MERGE = 0, CONV_DIM, 2 * CONV_DIM
RM_KS = RM_MERGE + 2 * D_MODEL
RM_KW = RM_KS + K_PAD_W
RM_KVC = RM_KW + K_PAD_W
RM_END = RM_KVC + 2 * KV_W
TR_VS = N_HEADS * HEAD_DIM
TR_VW = TR_VS + N_KV * V_ROWS
TR_NG = TR_VW + N_KV * V_ROWS
TR_END = TR_NG + 64


def _params(*sem):
    return pltpu.CompilerParams(dimension_semantics=sem, vmem_limit_bytes=VMEM_LIMIT)


def _rms(x, g):
    return x * lax.rsqrt(jnp.mean(x * x, axis=-1, keepdims=True) + EPS) * g


def _dot(a, b):
    return jnp.dot(a, b, preferred_element_type=F32)


def _col_reduce(reduce_fn, combine_fn, s):
    rows = [s[r:r + SUBLANES] for r in range(0, s.shape[0], SUBLANES)]
    while len(rows) > 1:
        paired = [combine_fn(rows[i], rows[i + 1]) for i in range(0, len(rows) - 1, 2)]
        rows = paired + rows[len(rows) - len(rows) % 2:]
    return reduce_fn(rows[0], axis=0, keepdims=True)


def _ffn_kernel(x_ref, g_ref, wi_ref, wo_ref, o_ref):
    x = x_ref[...]
    h = _rms(x, g_ref[...]).astype(wi_ref.dtype)
    acc = None
    for f0 in range(0, D_FF, TF):
        a = _dot(h, wi_ref[:, f0:f0 + TF])
        b = _dot(h, wi_ref[:, D_FF + f0:D_FF + f0 + TF])
        act = (a * jax.nn.sigmoid(a) * b).astype(wo_ref.dtype)
        part = _dot(act, wo_ref[f0:f0 + TF, :])
        acc = part if acc is None else acc + part
    o_ref[...] = x + 0.5 * acc


def _ffn(x, g, w_in, w_out):
    n = x.shape[0]
    const = lambda shape: pl.BlockSpec(shape, lambda i: (0,) * len(shape), pipeline_mode=pl.Buffered(1))
    return pl.pallas_call(
        _ffn_kernel,
        grid=(n // TM_FFN,),
        in_specs=[pl.BlockSpec((TM_FFN, D_MODEL), lambda i: (i, 0)), const((1, D_MODEL)),
                  const(w_in.shape), const(w_out.shape)],
        out_specs=pl.BlockSpec((TM_FFN, D_MODEL), lambda i: (i, 0)),
        out_shape=jax.ShapeDtypeStruct((n, D_MODEL), F32),
        compiler_params=_params("parallel"),
        name="ffn",
    )(x, g, w_in, w_out)


def _head_norm_rope(y, ones_bd, gain, cos, sin_lo, sin_hi):
    ss = _dot((y * y).astype(ones_bd.dtype), ones_bd)
    yn = y * lax.rsqrt(ss * (1.0 / HEAD_DIM) + EPS) * gain
    width = y.shape[-1]
    half = ROPE_DIM // 2
    return (yn * cos + pltpu.roll(yn, width - half, 1) * sin_lo + pltpu.roll(yn, half, 1) * sin_hi)


def _mix_proj_kernel(x_ref, g_ref, wr_ref, br_ref, wt_ref, bt_ref, kg_ref, qg_ref, bd_ref,
                     cos_ref, slo_ref, shi_ref, cost_ref, sint_ref,
                     hglu_ref, mg_ref, ks_ref, kw_ref, kvc_ref, qt_ref, vst_ref, vwt_ref, ngt_ref):
    h = _rms(x_ref[...], g_ref[...]).astype(wr_ref.dtype)
    tm = h.shape[0]

    def rows(lo, hi):
        return _dot(h, wr_ref[:, lo:hi]) + br_ref[:, lo:hi]

    hglu_ref[...] = rows(RM_GLU_A, RM_GLU_G) * jax.nn.sigmoid(rows(RM_GLU_G, RM_MERGE))
    mg_ref[...] = jax.nn.sigmoid(rows(RM_MERGE, RM_KS))
    for out_ref, base in ((ks_ref, RM_KS), (kw_ref, RM_KW)):
        for c0 in range(0, K_PAD_W, TN):
            y = rows(base + c0, base + c0 + TN)
            gain = kg_ref[:, base - RM_KS + c0:base - RM_KS + c0 + TN]
            out = _head_norm_rope(y, bd_ref[...], gain, cos_ref[...], slo_ref[...], shi_ref[...])
            out_ref[:, c0:c0 + TN] = out.astype(out_ref.dtype)
    kvc_ref[...] = rows(RM_KVC, RM_END)

    lanes = lambda ref, lo, hi: jnp.tile(ref[lo:hi, :], (1, tm // LANES))
    yt = lax.dot_general(wt_ref[...], h, (((1,), (1,)), ((), ())), preferred_element_type=F32)

    half = ROPE_DIM // 2
    cos_t, sin_t = cost_ref[...], sint_ref[...]
    for hd in range(N_HEADS):
        r0 = hd * HEAD_DIM
        y = yt[r0:r0 + HEAD_DIM] + lanes(bt_ref, r0, r0 + HEAD_DIM)
        ss = jnp.sum(y * y, axis=0, keepdims=True)
        yn = y * lax.rsqrt(ss * (1.0 / HEAD_DIM) + EPS) * lanes(qg_ref, r0, r0 + HEAD_DIM)
        x1, x2 = yn[0:half], yn[half:ROPE_DIM]
        rot = jnp.concatenate([x1 * cos_t - x2 * sin_t, x2 * cos_t + x1 * sin_t, yn[ROPE_DIM:]], axis=0)
        qt_ref[r0:r0 + HEAD_DIM, :] = rot.astype(qt_ref.dtype)

    vst_ref[...] = (yt[TR_VS:TR_VW] + lanes(bt_ref, TR_VS, TR_VW)).astype(vst_ref.dtype)
    vwt_ref[...] = (yt[TR_VW:TR_NG] + lanes(bt_ref, TR_VW, TR_NG)).astype(vwt_ref.dtype)
    n_gate = ngt_ref.shape[0]
    ngt_ref[...] = jax.nn.sigmoid(yt[TR_NG:TR_NG + n_gate] + lanes(bt_ref, TR_NG, TR_NG + n_gate))


def _mix_proj(x, g, w_rows, b_rows, w_t, b_t, k_gain, q_gain_t, ones_bd, cos, sin_lo, sin_hi, cos_t, sin_t, bsz, seq):
    n = x.shape[0]
    t_tiles = seq // TM_PROJ
    const = lambda shape: pl.BlockSpec(shape, lambda i: (0,) * len(shape), pipeline_mode=pl.Buffered(1))
    by_row = lambda width: pl.BlockSpec((TM_PROJ, width), lambda i: (i, 0))
    by_time = lambda n_rows: pl.BlockSpec((None, n_rows, TM_PROJ), lambda i: (i // t_tiles, 0, i % t_tiles))
    mx = w_rows.dtype
    n_gate = 3 * N_HEADS
    return pl.pallas_call(
        _mix_proj_kernel,
        grid=(n // TM_PROJ,),
        in_specs=[
            by_row(D_MODEL), const((1, D_MODEL)),
            const(w_rows.shape), const(b_rows.shape), const(w_t.shape), const(b_t.shape),
            const(k_gain.shape), const(q_gain_t.shape), const(ones_bd.shape),
            pl.BlockSpec((TM_PROJ, TN), lambda i: (i % t_tiles, 0)),
            pl.BlockSpec((TM_PROJ, TN), lambda i: (i % t_tiles, 0)),
            pl.BlockSpec((TM_PROJ, TN), lambda i: (i % t_tiles, 0)),
            pl.BlockSpec((ROPE_DIM // 2, TM_PROJ), lambda i: (0, i % t_tiles)),
            pl.BlockSpec((ROPE_DIM // 2, TM_PROJ), lambda i: (0, i % t_tiles)),
        ],
        out_specs=[
            by_row(CONV_DIM), by_row(2 * D_MODEL), by_row(K_PAD_W), by_row(K_PAD_W), by_row(2 * KV_W),
            by_time(N_HEADS * HEAD_DIM), by_time(N_KV * V_ROWS), by_time(N_KV * V_ROWS), by_time(n_gate),
        ],
        out_shape=[
            jax.ShapeDtypeStruct((n, CONV_DIM), F32), jax.ShapeDtypeStruct((n, 2 * D_MODEL), F32),
            jax.ShapeDtypeStruct((n, K_PAD_W), mx), jax.ShapeDtypeStruct((n, K_PAD_W), mx),
            jax.ShapeDtypeStruct((n, 2 * KV_W), F32),
            jax.ShapeDtypeStruct((bsz, N_HEADS * HEAD_DIM, seq), mx),
            jax.ShapeDtypeStruct((bsz, N_KV * V_ROWS, seq), mx), jax.ShapeDtypeStruct((bsz, N_KV * V_ROWS, seq), mx),
            jax.ShapeDtypeStruct((bsz, n_gate, seq), F32),
        ],
        compiler_params=_params("parallel"),
        name="mix_proj",
    )(x, g, w_rows, b_rows, w_t, b_t, k_gain, q_gain_t, ones_bd, cos, sin_lo, sin_hi, cos_t, sin_t)


def _conv_kernel(cur_ref, halo_ref, w_ref, b_ref, lg_ref, lb_ref, o_ref, sh_scr, conv_scr):
    t = pl.program_id(1)
    ext = TT + HALO - 8
    sh_scr[0, 0:HALO, :] = jnp.where(t == 0, 0.0, halo_ref[...])
    sh_scr[0, HALO:HALO + TT, :] = cur_ref[...]
    for r in range(1, 8):
        sh_scr[r, 0:ext, :] = sh_scr[0, r:r + ext, :]

    lead = HALO - (CONV_K - 1)

    def row_chunk(ri, carry):
        r0 = pl.multiple_of(ri * CONV_ROWS, CONV_ROWS)
        for c0 in range(0, CONV_DIM, CONV_COLS):
            acc = jnp.broadcast_to(b_ref[:, c0:c0 + CONV_COLS], (CONV_ROWS, CONV_COLS))
            for k in range(CONV_K):
                off = k + lead
                base = pl.multiple_of(r0 + (off // 8) * 8, 8)
                acc = acc + w_ref[k:k + 1, c0:c0 + CONV_COLS] * sh_scr[off % 8, pl.ds(base, CONV_ROWS),
                                                                       c0:c0 + CONV_COLS]
            conv_scr[pl.ds(r0, CONV_ROWS), c0:c0 + CONV_COLS] = acc
        return carry

    lax.fori_loop(0, TT // CONV_ROWS, row_chunk, 0)

    h = conv_scr[...]
    mu = jnp.mean(h, axis=-1, keepdims=True)
    var = jnp.mean(jnp.square(h - mu), axis=-1, keepdims=True)
    y = (h - mu) * lax.rsqrt(var + EPS) * lg_ref[...] + lb_ref[...]
    o_ref[...] = (y * jax.nn.sigmoid(y)).astype(o_ref.dtype)


def _conv_module(h, w, b, ln_g, ln_b, out_dtype):
    bsz, seq, _ = h.shape
    halo_per_tile = TT // HALO
    vec = pl.BlockSpec((1, CONV_DIM), lambda bi, t: (0, 0))
    return pl.pallas_call(
        _conv_kernel,
        grid=(bsz, seq // TT),
        in_specs=[
            pl.BlockSpec((None, TT, CONV_DIM), lambda bi, t: (bi, t, 0)),
            pl.BlockSpec((None, HALO, CONV_DIM), lambda bi, t: (bi, jnp.maximum(t * halo_per_tile - 1, 0), 0)),
            pl.BlockSpec((CONV_K, CONV_DIM), lambda bi, t: (0, 0)),
            vec, vec, vec,
        ],
        out_specs=pl.BlockSpec((None, TT, CONV_DIM), lambda bi, t: (bi, t, 0)),
        out_shape=jax.ShapeDtypeStruct((bsz, seq, CONV_DIM), out_dtype),
        scratch_shapes=[pltpu.VMEM((8, TT + HALO, CONV_DIM), F32), pltpu.VMEM((TT, CONV_DIM), F32)],
        compiler_params=_params("parallel", "parallel"),
        name="conv_module",
    )(h, h, w, b, ln_g, ln_b)


def _compress_kernel(a_ref, plo_ref, phi_ref, w1a_ref, w1b_ref, b1_ref, w2_ref,
                     hg_ref, cos_ref, slo_ref, shi_ref, o_ref, *, is_key):
    a = a_ref[...]
    n = a.shape[0]
    p = _dot((a + plo_ref[...]).astype(w1a_ref.dtype), w1a_ref[...])
    q = _dot((a + phi_ref[...]).astype(w1b_ref.dtype), w1b_ref[...])
    hid = p + pltpu.roll(q, n - 1, 0) + b1_ref[...]
    hid = jax.nn.gelu(hid, approximate=True)
    c = _dot(hid.astype(w2_ref.dtype), w2_ref[...])
    if is_key:
        ss = jnp.sum(c * c, axis=-1, keepdims=True)
        cn = c * lax.rsqrt(ss * (1.0 / HEAD_DIM) + EPS) * hg_ref[...]
        half = ROPE_DIM // 2
        c = cn * cos_ref[...] + pltpu.roll(cn, LANES - half, 1) * slo_ref[...] + pltpu.roll(cn, half, 1) * shi_ref[...]
    o_ref[...] = c[:, :HEAD_DIM].astype(o_ref.dtype)


def _compress(a, pos_lo, pos_hi, w1a, w1b, b1, w2, head_gain, cos, sin_lo, sin_hi, *, is_key, out_dtype):
    bsz, ng, n_sub, width = a.shape
    full = lambda *shape: pl.BlockSpec(shape, lambda bi, gi: (0,) * len(shape))
    return pl.pallas_call(
        functools.partial(_compress_kernel, is_key=is_key),
        grid=(bsz, ng),
        in_specs=[
            pl.BlockSpec((None, None, n_sub, width), lambda bi, gi: (bi, gi, 0, 0)),
            full(1, width), full(1, width), full(width, CMP_HID), full(width, CMP_HID),
            full(1, CMP_HID), full(CMP_HID, LANES), full(1, LANES),
            full(n_sub, LANES), full(n_sub, LANES), full(n_sub, LANES),
        ],
        out_specs=pl.BlockSpec((None, None, n_sub, HEAD_DIM), lambda bi, gi: (bi, gi, 0, 0)),
        out_shape=jax.ShapeDtypeStruct((bsz, ng, n_sub, HEAD_DIM), out_dtype),
        compiler_params=_params("parallel", "parallel"),
        name="compress_k" if is_key else "compress_v",
    )(a, pos_lo, pos_hi, w1a, w1b, b1, w2, head_gain, cos, sin_lo, sin_hi)


def _nsa_kernel(qt_ref, ks_ref, blk_ref, vst_ref, kw_ref, vwt_ref, kc_ref, vct_ref, c2s_ref, g_ref, o_ref,
                qaug_scr, s_scr):
    m_cols = HPG * TQ
    t0 = pl.program_id(2) * TQ
    qt = jnp.concatenate([qt_ref[h] for h in range(HPG)], axis=1)
    qt_pad = jnp.concatenate([qt, jnp.zeros((LANES - HEAD_DIM, m_cols), qt.dtype)], axis=0)
    col_t = t0 + (lax.broadcasted_iota(jnp.int32, (1, m_cols), 1) & (TQ - 1))

    n_cmp = kc_ref.shape[0]
    tq_io = lax.broadcasted_iota(jnp.int32, (TQ, 1), 0)
    t0a = pl.multiple_of(t0, TQ)
    w0 = pl.multiple_of(jnp.maximum(t0 - WINDOW, 0), TQ)
    s_cmp = _dot(kc_ref[...], qt)
    s_win = _dot(kw_ref[pl.ds(w0, WINDOW + TQ), :], qt_pad)
    s_own = _dot(ks_ref[pl.ds(t0a, TQ), :], qt_pad)
    s_first = [_dot(ks_ref[ahead * TK_SEL:(ahead + 1) * TK_SEL, :], qt_pad) for ahead in range(SEL_AHEAD)]

    s = s_cmp
    cmp_end = lax.broadcasted_iota(jnp.int32, (n_cmp, 1), 0) * CMP_STRIDE + (CMP_LEN - 1)
    cmask = cmp_end <= col_t
    s = jnp.where(cmask, s, NEG)
    p = jnp.exp2(s - jnp.maximum(_col_reduce(jnp.max, jnp.maximum, s), 0.1 * NEG))
    l = _col_reduce(jnp.sum, jnp.add, p)
    pc = p * (1.0 / jnp.where(l > 0.0, l, 1.0))
    o_c = _dot(vct_ref[...], pc.astype(vct_ref.dtype))

    s = s_win
    s = jnp.concatenate([jnp.where(w0 + tq_io > col_t - WINDOW, s[0:TQ], -MASK_BIG), s[TQ:]], axis=0)
    s = jnp.where(w0 + lax.broadcasted_iota(jnp.int32, (WINDOW + TQ, 1), 0) <= col_t, s, -MASK_BIG)
    p = jnp.exp2(s - _col_reduce(jnp.max, jnp.maximum, s))
    acc_w = _dot(vwt_ref[:, pl.ds(w0, WINDOW + TQ)], p.astype(vwt_ref.dtype))
    acc_w, l_w = acc_w[0:HEAD_DIM], acc_w[HEAD_DIM:HEAD_DIM + 1]

    pc_sum = pc[:, 0:TQ] + pc[:, TQ:2 * TQ] + pc[:, 2 * TQ:3 * TQ] + pc[:, 3 * TQ:4 * TQ]
    c2s = c2s_ref[...]
    hi = pc_sum.astype(c2s.dtype)
    lo = (pc_sum - hi.astype(F32)).astype(c2s.dtype)
    imp = _dot(c2s, hi) + _dot(c2s, lo)
    n_blk = imp.shape[0]
    j_io = lax.broadcasted_iota(jnp.int32, (n_blk, TQ), 0)
    t_io = t0 + lax.broadcasted_iota(jnp.int32, (n_blk, TQ), 1)
    cur = t_io // SEL_LEN
    forced = (j_io == 0) | (j_io == cur) | (j_io == cur - 1)
    valid = j_io * SEL_LEN <= t_io
    imp = jnp.where(valid, jnp.where(forced, 1e4, imp), -1.0)

    sel = jnp.where(forced & valid, 1.0, 0.0)
    imp = jnp.where(forced, -3e38, imp)
    j_f = j_io.astype(F32)
    for _ in range(max(min(N_SEL, n_blk) - 3, 0)):
        mx = jnp.max(imp, axis=0, keepdims=True)
        first = jnp.min(jnp.where(imp == mx, j_f, float(n_blk)), axis=0, keepdims=True)
        pick = j_f == first
        sel = jnp.where(pick & (mx >= 0.0), 1.0, sel)
        imp = jnp.where(pick, -3e38, imp)
    bias_f32 = (jnp.where(j_io == cur, 0.0, sel) - 1.0) * MASK_BIG
    bias = bias_f32.astype(qaug_scr.dtype)

    qaug_scr[0:LANES, :] = qt_pad
    if n_blk < LANES:
        qaug_scr[LANES + n_blk:, :] = jnp.zeros((LANES - n_blk, m_cols), qaug_scr.dtype)
    for h in range(HPG):
        qaug_scr[LANES:LANES + n_blk, h * TQ:(h + 1) * TQ] = bias

    def online(s, vt, carry):
        m_prev, acc = carry
        m_new = jnp.maximum(m_prev, _col_reduce(jnp.max, jnp.maximum, s))
        alpha = jnp.exp2(m_prev - m_new)
        p = jnp.exp2(s - m_new)
        return m_new, alpha * acc + _dot(vt, p.astype(vt.dtype))

    def finish(carry):
        _, acc = carry
        return acc[0:HEAD_DIM], acc[HEAD_DIM:HEAD_DIM + 1]

    init = (jnp.full((1, m_cols), -jnp.inf, F32), jnp.zeros((V_ROWS, m_cols), F32))

    own = ((t0 + tq_io) // SEL_LEN == col_t // SEL_LEN) & (t0 + tq_io <= col_t)
    s = jnp.where(own, s_own, -MASK_BIG)
    carry = online(s, vst_ref[:, pl.ds(t0a, TQ)], init)

    def sel_scores(tile):
        rows = pl.ds(pl.multiple_of(tile * TK_SEL, TK_SEL), TK_SEL)
        return _dot(jnp.concatenate([ks_ref[rows, :], blk_ref[rows, :]], axis=1), qaug_scr[...])

    last_tile = ks_ref.shape[0] // TK_SEL - 1
    blocks_per_tile = TK_SEL // SEL_LEN
    for ahead in range(SEL_AHEAD):
        per_block = [jnp.broadcast_to(bias_f32[j:j + 1, :], (SEL_LEN, TQ))
                     for j in range(ahead * blocks_per_tile, (ahead + 1) * blocks_per_tile)]
        s_scr[ahead] = s_first[ahead] + jnp.tile(jnp.concatenate(per_block, axis=0), (1, HPG))

    def sel_group(i, carry):
        first = SEL_GROUP * i
        scores = [s_scr[ahead] for ahead in range(SEL_AHEAD)]
        scores += [sel_scores(jnp.minimum(first + SEL_AHEAD + ahead, last_tile)) for ahead in range(SEL_AHEAD)]
        for part in range(SEL_GROUP):
            k0 = pl.multiple_of((first + part) * TK_SEL, TK_SEL)
            carry = online(scores[part], vst_ref[:, pl.ds(k0, TK_SEL)], carry)
            if part + 2 * SEL_AHEAD < SEL_GROUP + SEL_AHEAD:
                scores.append(sel_scores(jnp.minimum(first + part + 2 * SEL_AHEAD, last_tile)))
        for ahead in range(SEL_AHEAD):
            s_scr[ahead] = scores[SEL_GROUP + ahead]
        return carry

    acc_s, l_s = finish(lax.fori_loop(0, t0 // (SEL_GROUP * TK_SEL) + 1, sel_group, carry))

    gate = lambda c: jnp.concatenate([g_ref[c, h:h + 1, :] for h in range(HPG)], axis=1)
    out = gate(0) * o_c + (gate(1) * (1.0 / l_s)) * acc_s + (gate(2) * (1.0 / l_w)) * acc_w
    for h in range(HPG):
        o_ref[h] = out[:, h * TQ:(h + 1) * TQ].astype(o_ref.dtype)


def _nsa(qt, ks, blk, vst, kw, vwt, kc, vct, c2s, gates):
    bsz, ng, _, _, seq = qt.shape
    n_cmp = kc.shape[2]
    n_blk = c2s.shape[0]
    per_group = lambda *shape: pl.BlockSpec((None, None) + shape, lambda bi, gi, qi: (bi, gi) + (0,) * len(shape))
    q_tile = lambda *lead: pl.BlockSpec((None, None) + lead + (TQ,), lambda bi, gi, qi: (bi, gi) + (0,) * len(lead) + (qi,))
    key_slot = pl.BlockSpec((None, seq, LANES), lambda bi, gi, qi: (bi, 0, gi))
    return pl.pallas_call(
        _nsa_kernel,
        grid=(bsz, ng, seq // TQ),
        in_specs=[
            q_tile(HPG, HEAD_DIM),
            key_slot, pl.BlockSpec((seq, LANES), lambda bi, gi, qi: (0, 0)), per_group(V_ROWS, seq),
            key_slot, per_group(V_ROWS, seq),
            per_group(n_cmp, HEAD_DIM), per_group(HEAD_DIM, n_cmp),
            pl.BlockSpec((n_blk, n_cmp), lambda bi, gi, qi: (0, 0)),
            q_tile(3, HPG),
        ],
        out_specs=q_tile(HPG, HEAD_DIM),
        out_shape=jax.ShapeDtypeStruct((bsz, ng, HPG, HEAD_DIM, seq), qt.dtype),
        scratch_shapes=[pltpu.VMEM((2 * LANES, HPG * TQ), qt.dtype), pltpu.VMEM((SEL_AHEAD, TK_SEL, HPG * TQ), F32)],
        compiler_params=_params("parallel", "parallel", "arbitrary"),
        name="nsa",
    )(qt, ks, blk, vst, kw, vwt, kc, vct, c2s, gates)


def _merge_kernel(x_ref, hc_ref, aot_ref, gc_ref, gn_ref, wc_ref, wn_ref, wo_ref, o_ref):
    y_conv = _dot(hc_ref[...], wc_ref[...])
    y_nsa = lax.dot_general(aot_ref[...], wn_ref[...], (((0,), (0,)), ((), ())), preferred_element_type=F32)
    mixed = gc_ref[...] * y_conv + gn_ref[...] * y_nsa
    o_ref[...] = x_ref[...] + _dot(mixed.astype(wo_ref.dtype), wo_ref[...])


def _merge(x, hc, aot, gates, wc, wn, wo):
    n = x.shape[0]
    seq = aot.shape[2]
    t_tiles = seq // TM_OUT
    row = lambda j: pl.BlockSpec((TM_OUT, D_MODEL), lambda i: (i, j))
    w_spec = pl.BlockSpec((D_MODEL, D_MODEL), lambda i: (0, 0))
    aot_spec = pl.BlockSpec((None, D_MODEL, TM_OUT), lambda i: (i // t_tiles, 0, i % t_tiles))
    return pl.pallas_call(
        _merge_kernel,
        grid=(n // TM_OUT,),
        in_specs=[row(0), row(0), aot_spec, row(0), row(1), w_spec, w_spec, w_spec],
        out_specs=row(0),
        out_shape=jax.ShapeDtypeStruct((n, D_MODEL), F32),
        compiler_params=_params("parallel"),
        name="merge_out",
    )(x, hc, aot, gates, gates, wc, wn, wo)


def _ple_kernel(x_ref, p_ref, g_ref, gp_ref, wg_ref, wp_ref, o_ref):
    x = x_ref[...]
    e = _rms(_dot(p_ref[...].astype(wp_ref.dtype), wp_ref[...]), gp_ref[...])
    gate = jax.nn.sigmoid(_dot(_rms(x, g_ref[...]).astype(wg_ref.dtype), wg_ref[...]))
    o_ref[...] = x + gate * e


def _ple(x, p, g, g_post, w_gate, w_proj):
    n = x.shape[0]
    vec = pl.BlockSpec((1, D_MODEL), lambda i: (0, 0))
    return pl.pallas_call(
        _ple_kernel,
        grid=(n // TM_OUT,),
        in_specs=[
            pl.BlockSpec((TM_OUT, D_MODEL), lambda i: (i, 0)),
            pl.BlockSpec((TM_OUT, PLE_DIM), lambda i: (i, 0)),
            vec, vec,
            pl.BlockSpec((D_MODEL, D_MODEL), lambda i: (0, 0)),
            pl.BlockSpec((PLE_DIM, D_MODEL), lambda i: (0, 0)),
        ],
        out_specs=pl.BlockSpec((TM_OUT, D_MODEL), lambda i: (i, 0)),
        out_shape=jax.ShapeDtypeStruct((n, D_MODEL), F32),
        compiler_params=_params("parallel"),
        name="ple",
    )(x, p, g, g_post, w_gate, w_proj)


def _rope_tables(pos, width):
    half = ROPE_DIM // 2
    inv = ROPE_THETA ** (-jnp.arange(half, dtype=F32) / half)
    ang = pos.astype(F32)[:, None] * inv[None, :]
    cos, sin = jnp.cos(ang), jnp.sin(ang)
    n = pos.shape[0]
    rest = HEAD_DIM - ROPE_DIM
    seg_cos = jnp.concatenate([cos, cos, jnp.ones((n, rest), F32)], axis=1)
    seg_lo = jnp.concatenate([-sin, jnp.zeros((n, HEAD_DIM - half), F32)], axis=1)
    seg_hi = jnp.concatenate([jnp.zeros((n, half), F32), sin, jnp.zeros((n, rest), F32)], axis=1)
    reps = width // HEAD_DIM
    return tuple(jnp.tile(t, (1, reps)) for t in (seg_cos, seg_lo, seg_hi))


def kernel(x, p, ffn1_norm, ffn1_w_in, ffn1_w_out, mix_norm, w_in, b_in, conv_w, conv_b, conv_ln_g, conv_ln_b, conv_w_out, q_norm, k_norm, cmp_pos, cmp_w1, cmp_b1, cmp_w2, nsa_w_out, w_out, ffn2_norm, ffn2_w_in, ffn2_w_out, ple_norm, ple_w_gate, ple_w_proj, ple_post_norm):
    bsz, seq, _ = x.shape
    depth = w_in.shape[0]
    n_rows = bsz * seq
    n_sub = seq // CMP_STRIDE
    n_blk = seq // SEL_LEN
    mx = MXU_DTYPE
    row = lambda v: v[:, None, :]

    def kv_cols(j):
        return slice(OFF_KV + j * KV_W, OFF_KV + (j + 1) * KV_W)

    def group_slots(a, width, fill=None):
        a = a.reshape(a.shape[:-1] + (N_KV, HEAD_DIM))
        tail = jnp.zeros(a.shape[:-1] + (width - HEAD_DIM,), a.dtype)
        if fill is not None:
            tail = tail.at[..., 0].set(fill)
        a = jnp.concatenate([a, tail], axis=-1)
        return a.reshape(a.shape[:-2] + (N_KV * width,))

    def gate_order(a):
        a = a.reshape(a.shape[:-1] + (N_KV, HPG, 3))
        return jnp.swapaxes(a, -1, -2).reshape(a.shape[:-3] + (3 * N_HEADS,))

    def split(w, transposed_fill):
        rows_part = jnp.concatenate(
            [w[..., 0:OFF_Q], w[..., OFF_MERGE:N_IN], group_slots(w[..., kv_cols(2)], LANES),
             group_slots(w[..., kv_cols(4)], LANES), w[..., kv_cols(0)], w[..., kv_cols(1)]], axis=-1)
        gates_part = gate_order(w[..., OFF_GATE:OFF_MERGE])
        pad = jnp.zeros(w.shape[:-1] + (TR_END - TR_NG - 3 * N_HEADS,), w.dtype)
        t_part = jnp.concatenate(
            [w[..., OFF_Q:OFF_KV], group_slots(w[..., kv_cols(3)], V_ROWS, transposed_fill),
             group_slots(w[..., kv_cols(5)], V_ROWS, transposed_fill), gates_part, pad], axis=-1)
        return rows_part, t_part

    w_rows, w_t = split(w_in, None)
    w_rows, w_t = w_rows.astype(mx), jnp.swapaxes(w_t, 1, 2).astype(mx)
    b_rows, b_t = split(b_in, 1.0)
    lane_rep = lambda v: jnp.broadcast_to(v[:, :, None], v.shape + (LANES,))
    b_rows, b_t = row(b_rows), lane_rep(b_t)

    q_gain_t = lane_rep(jnp.tile(q_norm, (1, N_HEADS)) * (HEAD_DIM ** -0.5 * LOG2_E))
    k_gain = row(jnp.concatenate([group_slots(jnp.tile(k_norm[:, 1], (1, N_KV)), LANES),
                                  group_slots(jnp.tile(k_norm[:, 2], (1, N_KV)), LANES)], axis=-1))
    cmp_gain = jnp.concatenate([k_norm[:, 0], jnp.zeros((depth, LANES - HEAD_DIM), F32)], axis=-1)[:, None, :]
    w2_pad = jnp.concatenate([cmp_w2, jnp.zeros(cmp_w2.shape[:-1] + (LANES - HEAD_DIM,), F32)], axis=-1).astype(mx)
    sub_w = CMP_STRIDE * HEAD_DIM
    pos_flat = cmp_pos.reshape(depth, 2, 2, 1, sub_w)

    layers = dict(
        ffn1_norm=row(ffn1_norm), ffn1_w_in=ffn1_w_in.astype(mx), ffn1_w_out=ffn1_w_out.astype(mx),
        mix_norm=row(mix_norm), w_rows=w_rows, b_rows=b_rows, w_t=w_t, b_t=b_t, k_gain=k_gain, q_gain_t=q_gain_t,
        conv_w=conv_w, conv_b=row(conv_b), conv_ln_g=row(conv_ln_g), conv_ln_b=row(conv_ln_b),
        conv_w_out=conv_w_out.astype(mx), cmp_gain=cmp_gain, pos_flat=pos_flat,
        cmp_w1=cmp_w1.astype(mx), cmp_b1=cmp_b1[:, :, None, :], cmp_w2=w2_pad,
        nsa_w_out=nsa_w_out.astype(mx), w_out=w_out.astype(mx),
        ffn2_norm=row(ffn2_norm), ffn2_w_in=ffn2_w_in.astype(mx), ffn2_w_out=ffn2_w_out.astype(mx),
        ple_norm=row(ple_norm), ple_w_gate=ple_w_gate.astype(mx), ple_w_proj=ple_w_proj.astype(mx),
        ple_post_norm=row(ple_post_norm), p=p.reshape(depth, n_rows, PLE_DIM),
    )

    tok_cos, tok_lo, tok_hi = _rope_tables(jnp.arange(seq), TN)
    cmp_cos, cmp_lo, cmp_hi = _rope_tables(jnp.arange(n_sub) * CMP_STRIDE + CMP_LEN - 1, LANES)
    seg = jnp.arange(TN) // HEAD_DIM
    ones_bd = (seg[:, None] == seg[None, :]).astype(mx)
    ci = jnp.arange(n_sub)[None, :]
    sj = jnp.arange(n_blk)[:, None]
    c2s = ((ci * CMP_STRIDE < (sj + 1) * SEL_LEN) & (ci * CMP_STRIDE + CMP_LEN > sj * SEL_LEN)).astype(mx)
    blk_of_key = ((jnp.arange(seq) // SEL_LEN)[:, None] == jnp.arange(LANES)[None, :]).astype(mx)
    half = ROPE_DIM // 2
    ang_t = (ROPE_THETA ** (-jnp.arange(half, dtype=F32) / half))[:, None] * jnp.arange(seq).astype(F32)[None, :]
    cos_t, sin_t = jnp.cos(ang_t), jnp.sin(ang_t)

    def sub_blocks(a):
        a = a.reshape(bsz, n_sub, CMP_STRIDE, N_KV, HEAD_DIM).transpose(0, 3, 1, 2, 4)
        return a.reshape(bsz, N_KV, n_sub, sub_w)

    def layer(xf, lp):
        xf = _ffn(xf, lp["ffn1_norm"], lp["ffn1_w_in"], lp["ffn1_w_out"])

        h_glu, merge_gates, ks, kw, plain, qt, vst, vwt, gates = _mix_proj(
            xf, lp["mix_norm"], lp["w_rows"], lp["b_rows"], lp["w_t"], lp["b_t"], lp["k_gain"], lp["q_gain_t"],
            ones_bd, tok_cos, tok_lo, tok_hi, cos_t, sin_t, bsz, seq)

        hc = _conv_module(h_glu.reshape(bsz, seq, CONV_DIM), lp["conv_w"], lp["conv_b"],
                          lp["conv_ln_g"], lp["conv_ln_b"], mx).reshape(n_rows, CONV_DIM)

        comp = []
        for j, is_key in ((0, True), (1, False)):
            pos = lp["pos_flat"][j]
            comp.append(_compress(
                sub_blocks(plain[:, j * KV_W:(j + 1) * KV_W]), pos[0], pos[1],
                lp["cmp_w1"][j, :sub_w], lp["cmp_w1"][j, sub_w:], lp["cmp_b1"][j], lp["cmp_w2"][j],
                lp["cmp_gain"], cmp_cos, cmp_lo, cmp_hi, is_key=is_key, out_dtype=mx))
        kc = comp[0]
        vct = comp[1].transpose(0, 1, 3, 2)

        aot = _nsa(qt.reshape(bsz, N_KV, HPG, HEAD_DIM, seq), ks.reshape(bsz, seq, K_PAD_W), blk_of_key,
                   vst.reshape(bsz, N_KV, V_ROWS, seq), kw.reshape(bsz, seq, K_PAD_W),
                   vwt.reshape(bsz, N_KV, V_ROWS, seq), kc, vct, c2s,
                   gates.reshape(bsz, N_KV, 3, HPG, seq))
        aot = aot.reshape(bsz, N_HEADS * HEAD_DIM, seq)

        xf = _merge(xf, hc, aot, merge_gates, lp["conv_w_out"], lp["nsa_w_out"], lp["w_out"])
        xf = _ffn(xf, lp["ffn2_norm"], lp["ffn2_w_in"], lp["ffn2_w_out"])
        xf = _ple(xf, lp["p"], lp["ple_norm"], lp["ple_post_norm"], lp["ple_w_gate"], lp["ple_w_proj"])
        return xf, None

    out, _ = lax.scan(layer, x.reshape(n_rows, D_MODEL), layers)
    return out.reshape(bsz, seq, D_MODEL)
```

```python
import functools

import jax
import jax.numpy as jnp
from jax import lax
from jax.experimental import pallas as pl
from jax.experimental.pallas import tpu as pltpu

D_MODEL = 1024
PLE_DIM = 256
D_FF = 2816
CONV_DIM = D_MODEL
CONV_K = 31
N_HEADS = 16
N_KV = 4
HPG = N_HEADS // N_KV
HEAD_DIM = 64
ROPE_DIM = HEAD_DIM // 4
ROPE_THETA = 500000.0
CMP_LEN = 32
CMP_STRIDE = 16
CMP_HID = 256
SEL_LEN = 64
N_SEL = 16
WINDOW = 512
NEG = -1e30
EPS = 1e-6
KV_W = N_KV * HEAD_DIM
OFF_Q = 2 * CONV_DIM
OFF_KV = OFF_Q + N_HEADS * HEAD_DIM
OFF_GATE = OFF_KV + 6 * KV_W
OFF_MERGE = OFF_GATE + 3 * N_HEADS
N_IN = OFF_MERGE + 2 * D_MODEL

MXU_DTYPE = jnp.bfloat16
F32 = jnp.float32
LANES = 128
SUBLANES = 8
VMEM_LIMIT = 56 * 1024 * 1024

TM_FFN = 512
TN = 256
TF = 256
TM_OUT = 512
TT = 256
HALO = 32
CONV_ROWS = 32
CONV_COLS = 512
TQ = 128
TK_SEL = 256
SEL_GROUP = 4
SEL_AHEAD = 4
SEL_BURST = 2
MASK_BIG = 2.0 ** 100
LOG2_E = 1.4426950408889634
V_ROWS = HEAD_DIM + 16

TM_PROJ = 512
K_PAD_W = N_KV * LANES
RM_GLU_A, RM_GLU_G, RM_MERGE = 0, CONV_DIM, 2 * CONV_DIM
RM_KS = RM_MERGE + 2 * D_MODEL
RM_KW = RM_KS + K_PAD_W
RM_KVC = RM_KW + K_PAD_W
RM_END = RM_KVC + 2 * KV_W
TR_VS = N_HEADS * HEAD_DIM
TR_VW = TR_VS + N_KV * V_ROWS
TR_NG = TR_VW + N_KV * V_ROWS
TR_END = TR_NG + 64


def _params(*sem):
    return pltpu.CompilerParams(dimension_semantics=sem, vmem_limit_bytes=VMEM_LIMIT)


def _rms(x, g):
    return x * lax.rsqrt(jnp.mean(x * x, axis=-1, keepdims=True) + EPS) * g


def _dot(a, b):
    return jnp.dot(a, b, preferred_element_type=F32)


def _col_reduce(reduce_fn, combine_fn, s):
    rows = [s[r:r + SUBLANES] for r in range(0, s.shape[0], SUBLANES)]
    while len(rows) > 1:
        paired = [combine_fn(rows[i], rows[i + 1]) for i in range(0, len(rows) - 1, 2)]
        rows = paired + rows[len(rows) - len(rows) % 2:]
    return reduce_fn(rows[0], axis=0, keepdims=True)


def _ffn_kernel(x_ref, g_ref, wi_ref, wo_ref, o_ref):
    x = x_ref[...]
    h = _rms(x, g_ref[...]).astype(wi_ref.dtype)
    acc = None
    for f0 in range(0, D_FF, TF):
        a = _dot(h, wi_ref[:, f0:f0 + TF])
        b = _dot(h, wi_ref[:, D_FF + f0:D_FF + f0 + TF])
        act = (a * jax.nn.sigmoid(a) * b).astype(wo_ref.dtype)
        part = _dot(act, wo_ref[f0:f0 + TF, :])
        acc = part if acc is None else acc + part
    o_ref[...] = x + 0.5 * acc


def _ffn(x, g, w_in, w_out):
    n = x.shape[0]
    const = lambda shape: pl.BlockSpec(shape, lambda i: (0,) * len(shape), pipeline_mode=pl.Buffered(1))
    return pl.pallas_call(
        _ffn_kernel,
        grid=(n // TM_FFN,),
        in_specs=[pl.BlockSpec((TM_FFN, D_MODEL), lambda i: (i, 0)), const((1, D_MODEL)),
                  const(w_in.shape), const(w_out.shape)],
        out_specs=pl.BlockSpec((TM_FFN, D_MODEL), lambda i: (i, 0)),
        out_shape=jax.ShapeDtypeStruct((n, D_MODEL), F32),
        compiler_params=_params("parallel"),
        name="ffn",
    )(x, g, w_in, w_out)


def _head_norm_rope(y, ones_bd, gain, cos, sin_lo, sin_hi):
    ss = _dot((y * y).astype(ones_bd.dtype), ones_bd)
    yn = y * lax.rsqrt(ss * (1.0 / HEAD_DIM) + EPS) * gain
    width = y.shape[-1]
    half = ROPE_DIM // 2
    return (yn * cos + pltpu.roll(yn, width - half, 1) * sin_lo + pltpu.roll(yn, half, 1) * sin_hi)


def _mix_proj_kernel(x_ref, g_ref, wr_ref, br_ref, wt_ref, bt_ref, kg_ref, qg_ref, bd_ref,
                     cos_ref, slo_ref, shi_ref, cost_ref, sint_ref,
                     hglu_ref, mg_ref, ks_ref, kw_ref, kvc_ref, qt_ref, vst_ref, vwt_ref, ngt_ref):
    h = _rms(x_ref[...], g_ref[...]).astype(wr_ref.dtype)
    tm = h.shape[0]

    def rows(lo, hi):
        return _dot(h, wr_ref[:, lo:hi]) + br_ref[:, lo:hi]

    hglu_ref[...] = rows(RM_GLU_A, RM_GLU_G) * jax.nn.sigmoid(rows(RM_GLU_G, RM_MERGE))
    mg_ref[...] = jax.nn.sigmoid(rows(RM_MERGE, RM_KS))
    for out_ref, base in ((ks_ref, RM_KS), (kw_ref, RM_KW)):
        for c0 in range(0, K_PAD_W, TN):
            y = rows(base + c0, base + c0 + TN)
            gain = kg_ref[:, base - RM_KS + c0:base - RM_KS + c0 + TN]
            out = _head_norm_rope(y, bd_ref[...], gain, cos_ref[...], slo_ref[...], shi_ref[...])
            out_ref[:, c0:c0 + TN] = out.astype(out_ref.dtype)
    kvc_ref[...] = rows(RM_KVC, RM_END)

    lanes = lambda ref, lo, hi: jnp.tile(ref[lo:hi, :], (1, tm // LANES))
    yt = lax.dot_general(wt_ref[...], h, (((1,), (1,)), ((), ())), preferred_element_type=F32)

    half = ROPE_DIM // 2
    cos_t, sin_t = cost_ref[...], sint_ref[...]
    for hd in range(N_HEADS):
        r0 = hd * HEAD_DIM
        y = yt[r0:r0 + HEAD_DIM] + lanes(bt_ref, r0, r0 + HEAD_DIM)
        ss = jnp.sum(y * y, axis=0, keepdims=True)
        yn = y * lax.rsqrt(ss * (1.0 / HEAD_DIM) + EPS) * lanes(qg_ref, r0, r0 + HEAD_DIM)
        x1, x2 = yn[0:half], yn[half:ROPE_DIM]
        rot = jnp.concatenate([x1 * cos_t - x2 * sin_t, x2 * cos_t + x1 * sin_t, yn[ROPE_DIM:]], axis=0)
        qt_ref[r0:r0 + HEAD_DIM, :] = rot.astype(qt_ref.dtype)

    vst_ref[...] = (yt[TR_VS:TR_VW] + lanes(bt_ref, TR_VS, TR_VW)).astype(vst_ref.dtype)
    vwt_ref[...] = (yt[TR_VW:TR_NG] + lanes(bt_ref, TR_VW, TR_NG)).astype(vwt_ref.dtype)
    n_gate = ngt_ref.shape[0]
    ngt_ref[...] = jax.nn.sigmoid(yt[TR_NG:TR_NG + n_gate] + lanes(bt_ref, TR_NG, TR_NG + n_gate))


def _mix_proj(x, g, w_rows, b_rows, w_t, b_t, k_gain, q_gain_t, ones_bd, cos, sin_lo, sin_hi, cos_t, sin_t, bsz, seq):
    n = x.shape[0]
    t_tiles = seq // TM_PROJ
    const = lambda shape: pl.BlockSpec(shape, lambda i: (0,) * len(shape), pipeline_mode=pl.Buffered(1))
    by_row = lambda width: pl.BlockSpec((TM_PROJ, width), lambda i: (i, 0))
    by_time = lambda n_rows: pl.BlockSpec((None, n_rows, TM_PROJ), lambda i: (i // t_tiles, 0, i % t_tiles))
    mx = w_rows.dtype
    n_gate = 3 * N_HEADS
    return pl.pallas_call(
        _mix_proj_kernel,
        grid=(n // TM_PROJ,),
        in_specs=[
            by_row(D_MODEL), const((1, D_MODEL)),
            const(w_rows.shape), const(b_rows.shape), const(w_t.shape), const(b_t.shape),
            const(k_gain.shape), const(q_gain_t.shape), const(ones_bd.shape),
            pl.BlockSpec((TM_PROJ, TN), lambda i: (i % t_tiles, 0)),
            pl.BlockSpec((TM_PROJ, TN), lambda i: (i % t_tiles, 0)),
            pl.BlockSpec((TM_PROJ, TN), lambda i: (i % t_tiles, 0)),
            pl.BlockSpec((ROPE_DIM // 2, TM_PROJ), lambda i: (0, i % t_tiles)),
            pl.BlockSpec((ROPE_DIM // 2, TM_PROJ), lambda i: (0, i % t_tiles)),
        ],
        out_specs=[
            by_row(CONV_DIM), by_row(2 * D_MODEL), by_row(K_PAD_W), by_row(K_PAD_W), by_row(2 * KV_W),
            by_time(N_HEADS * HEAD_DIM), by_time(N_KV * V_ROWS), by_time(N_KV * V_ROWS), by_time(n_gate),
        ],
        out_shape=[
            jax.ShapeDtypeStruct((n, CONV_DIM), F32), jax.ShapeDtypeStruct((n, 2 * D_MODEL), F32),
            jax.ShapeDtypeStruct((n, K_PAD_W), mx), jax.ShapeDtypeStruct((n, K_PAD_W), mx),
            jax.ShapeDtypeStruct((n, 2 * KV_W), F32),
            jax.ShapeDtypeStruct((bsz, N_HEADS * HEAD_DIM, seq), mx),
            jax.ShapeDtypeStruct((bsz, N_KV * V_ROWS, seq), mx), jax.ShapeDtypeStruct((bsz, N_KV * V_ROWS, seq), mx),
            jax.ShapeDtypeStruct((bsz, n_gate, seq), F32),
        ],
        compiler_params=_params("parallel"),
        name="mix_proj",
    )(x, g, w_rows, b_rows, w_t, b_t, k_gain, q_gain_t, ones_bd, cos, sin_lo, sin_hi, cos_t, sin_t)


def _conv_kernel(cur_ref, halo_ref, w_ref, b_ref, lg_ref, lb_ref, o_ref, sh_scr, conv_scr):
    t = pl.program_id(1)
    ext = TT + HALO - 8
    sh_scr[0, 0:HALO, :] = jnp.where(t == 0, 0.0, halo_ref[...])
    sh_scr[0, HALO:HALO + TT, :] = cur_ref[...]
    for r in range(1, 8):
        sh_scr[r, 0:ext, :] = sh_scr[0, r:r + ext, :]

    lead = HALO - (CONV_K - 1)

    def row_chunk(ri, carry):
        r0 = pl.multiple_of(ri * CONV_ROWS, CONV_ROWS)
        for c0 in range(0, CONV_DIM, CONV_COLS):
            groups = CONV_ROWS // SUBLANES
            acc = jnp.broadcast_to(b_ref[:, c0:c0 + CONV_COLS], (groups, SUBLANES, CONV_COLS))
            for k in range(CONV_K):
                off = k + lead
                base = pl.multiple_of(r0 + (off // 8) * 8, 8)
                rows = sh_scr[off % 8, pl.ds(base, CONV_ROWS), c0:c0 + CONV_COLS]
                acc = acc + w_ref[k, :, c0:c0 + CONV_COLS] * rows.reshape(groups, SUBLANES, CONV_COLS)
            conv_scr[pl.ds(r0, CONV_ROWS), c0:c0 + CONV_COLS] = acc.reshape(CONV_ROWS, CONV_COLS)
        return carry

    lax.fori_loop(0, TT // CONV_ROWS, row_chunk, 0)

    h = conv_scr[...]
    mu = jnp.mean(h, axis=-1, keepdims=True)
    var = jnp.mean(jnp.square(h - mu), axis=-1, keepdims=True)
    y = (h - mu) * lax.rsqrt(var + EPS) * lg_ref[...] + lb_ref[...]
    o_ref[...] = (y * jax.nn.sigmoid(y)).astype(o_ref.dtype)


def _conv_module(h, w, b, ln_g, ln_b, out_dtype):
    bsz, seq, _ = h.shape
    halo_per_tile = TT // HALO
    vec = pl.BlockSpec((1, CONV_DIM), lambda bi, t: (0, 0))
    return pl.pallas_call(
        _conv_kernel,
        grid=(bsz, seq // TT),
        in_specs=[
            pl.BlockSpec((None, TT, CONV_DIM), lambda bi, t: (bi, t, 0)),
            pl.BlockSpec((None, HALO, CONV_DIM), lambda bi, t: (bi, jnp.maximum(t * halo_per_tile - 1, 0), 0)),
            pl.BlockSpec((CONV_K, SUBLANES, CONV_DIM), lambda bi, t: (0, 0, 0)),
            vec, vec, vec,
        ],
        out_specs=pl.BlockSpec((None, TT, CONV_DIM), lambda bi, t: (bi, t, 0)),
        out_shape=jax.ShapeDtypeStruct((bsz, seq, CONV_DIM), out_dtype),
        scratch_shapes=[pltpu.VMEM((8, TT + HALO, CONV_DIM), F32), pltpu.VMEM((TT, CONV_DIM), F32)],
        compiler_params=_params("parallel", "parallel"),
        name="conv_module",
    )(h, h, w, b, ln_g, ln_b)


def _compress_kernel(a_ref, plo_ref, phi_ref, w1a_ref, w1b_ref, b1_ref, w2_ref,
                     hg_ref, cos_ref, slo_ref, shi_ref, o_ref, *, is_key):
    a = a_ref[...]
    n = a.shape[0]
    p = _dot((a + plo_ref[...]).astype(w1a_ref.dtype), w1a_ref[...])
    q = _dot((a + phi_ref[...]).astype(w1b_ref.dtype), w1b_ref[...])
    hid = p + pltpu.roll(q, n - 1, 0) + b1_ref[...]
    hid = jax.nn.gelu(hid, approximate=True)
    c = _dot(hid.astype(w2_ref.dtype), w2_ref[...])
    if is_key:
        ss = jnp.sum(c * c, axis=-1, keepdims=True)
        cn = c * lax.rsqrt(ss * (1.0 / HEAD_DIM) + EPS) * hg_ref[...]
        half = ROPE_DIM // 2
        c = cn * cos_ref[...] + pltpu.roll(cn, LANES - half, 1) * slo_ref[...] + pltpu.roll(cn, half, 1) * shi_ref[...]
    o_ref[...] = c[:, :HEAD_DIM].astype(o_ref.dtype)


def _compress(a, pos_lo, pos_hi, w1a, w1b, b1, w2, head_gain, cos, sin_lo, sin_hi, *, is_key, out_dtype):
    bsz, ng, n_sub, width = a.shape
    full = lambda *shape: pl.BlockSpec(shape, lambda bi, gi: (0,) * len(shape))
    return pl.pallas_call(
        functools.partial(_compress_kernel, is_key=is_key),
        grid=(bsz, ng),
        in_specs=[
            pl.BlockSpec((None, None, n_sub, width), lambda bi, gi: (bi, gi, 0, 0)),
            full(1, width), full(1, width), full(width, CMP_HID), full(width, CMP_HID),
            full(1, CMP_HID), full(CMP_HID, LANES), full(1, LANES),
            full(n_sub, LANES), full(n_sub, LANES), full(n_sub, LANES),
        ],
        out_specs=pl.BlockSpec((None, None, n_sub, HEAD_DIM), lambda bi, gi: (bi, gi, 0, 0)),
        out_shape=jax.ShapeDtypeStruct((bsz, ng, n_sub, HEAD_DIM), out_dtype),
        compiler_params=_params("parallel", "parallel"),
        name="compress_k" if is_key else "compress_v",
    )(a, pos_lo, pos_hi, w1a, w1b, b1, w2, head_gain, cos, sin_lo, sin_hi)


def _nsa_kernel(qt_ref, ks_ref, blk_ref, vst_ref, kw_ref, vwt_ref, kc_ref, vct_ref, c2s_ref, g_ref, o_ref,
                qaug_scr, s_scr):
    m_cols = HPG * TQ
    t0 = pl.program_id(2) * TQ
    qt = jnp.concatenate([qt_ref[h] for h in range(HPG)], axis=1)
    qt_pad = jnp.concatenate([qt, jnp.zeros((LANES - HEAD_DIM, m_cols), qt.dtype)], axis=0)
    col_t = t0 + (lax.broadcasted_iota(jnp.int32, (1, m_cols), 1) & (TQ - 1))

    n_cmp = kc_ref.shape[0]
    tq_io = lax.broadcasted_iota(jnp.int32, (TQ, 1), 0)
    t0a = pl.multiple_of(t0, TQ)
    w0 = pl.multiple_of(jnp.maximum(t0 - WINDOW, 0), TQ)
    s_cmp = _dot(kc_ref[...], qt)
    s_win = _dot(kw_ref[pl.ds(w0, WINDOW + TQ), :], qt_pad)
    s_own = _dot(ks_ref[pl.ds(t0a, TQ), :], qt_pad)
    s_first = [_dot(ks_ref[ahead * TK_SEL:(ahead + 1) * TK_SEL, :], qt_pad) for ahead in range(SEL_AHEAD)]

    s = s_cmp
    cmp_end = lax.broadcasted_iota(jnp.int32, (n_cmp, 1), 0) * CMP_STRIDE + (CMP_LEN - 1)
    cmask = cmp_end <= col_t
    s = jnp.where(cmask, s, NEG)
    p = jnp.exp2(s - jnp.maximum(_col_reduce(jnp.max, jnp.maximum, s), 0.1 * NEG))
    l = _col_reduce(jnp.sum, jnp.add, p)
    pc = p * (1.0 / jnp.where(l > 0.0, l, 1.0))
    o_c = _dot(vct_ref[...], pc.astype(vct_ref.dtype))

    pc_sum = pc[:, 0:TQ] + pc[:, TQ:2 * TQ] + pc[:, 2 * TQ:3 * TQ] + pc[:, 3 * TQ:4 * TQ]
    c2s = c2s_ref[...]
    hi = pc_sum.astype(c2s.dtype)
    lo = (pc_sum - hi.astype(F32)).astype(c2s.dtype)
    imp = _dot(c2s, hi) + _dot(c2s, lo)
    n_blk = imp.shape[0]
    j_io = lax.broadcasted_iota(jnp.int32, (n_blk, TQ), 0)
    t_io = t0 + lax.broadcasted_iota(jnp.int32, (n_blk, TQ), 1)
    cur = t_io // SEL_LEN
    forced = (j_io == 0) | (j_io == cur) | (j_io == cur - 1)
    valid = j_io * SEL_LEN <= t_io
    imp = jnp.where(valid, jnp.where(forced, 1e4, imp), -1.0)

    sel = jnp.where(forced & valid, 1.0, 0.0)
    imp = jnp.where(forced, -3e38, imp)
    j_f = j_io.astype(F32)
    for _ in range(max(min(N_SEL, n_blk) - 3, 0)):
        mx = jnp.max(imp, axis=0, keepdims=True)
        first = jnp.min(jnp.where(imp == mx, j_f, float(n_blk)), axis=0, keepdims=True)
        pick = j_f == first
        sel = jnp.where(pick & (mx >= 0.0), 1.0, sel)
        imp = jnp.where(pick, -3e38, imp)
    bias_f32 = (jnp.where(j_io == cur, 0.0, sel) - 1.0) * MASK_BIG
    bias = bias_f32.astype(qaug_scr.dtype)

    qaug_scr[0:LANES, :] = qt_pad
    if n_blk < LANES:
        qaug_scr[LANES + n_blk:, :] = jnp.zeros((LANES - n_blk, m_cols), qaug_scr.dtype)
    for h in range(HPG):
        qaug_scr[LANES:LANES + n_blk, h * TQ:(h + 1) * TQ] = bias

    s = s_win
    s = jnp.concatenate([jnp.where(w0 + tq_io > col_t - WINDOW, s[0:TQ], -MASK_BIG), s[TQ:]], axis=0)
    s = jnp.where(w0 + lax.broadcasted_iota(jnp.int32, (WINDOW + TQ, 1), 0) <= col_t, s, -MASK_BIG)
    p = jnp.exp2(s - _col_reduce(jnp.max, jnp.maximum, s))
    acc_w = _dot(vwt_ref[:, pl.ds(w0, WINDOW + TQ)], p.astype(vwt_ref.dtype))
    acc_w, l_w = acc_w[0:HEAD_DIM], acc_w[HEAD_DIM:HEAD_DIM + 1]

    def online(s, vt, carry):
        m_prev, acc = carry
        m_new = jnp.maximum(m_prev, _col_reduce(jnp.max, jnp.maximum, s))
        alpha = jnp.exp2(m_prev - m_new)
        p = jnp.exp2(s - m_new)
        return m_new, alpha * acc + _dot(vt, p.astype(vt.dtype))

    def finish(carry):
        _, acc = carry
        return acc[0:HEAD_DIM], acc[HEAD_DIM:HEAD_DIM + 1]

    init = (jnp.full((1, m_cols), -jnp.inf, F32), jnp.zeros((V_ROWS, m_cols), F32))

    own = ((t0 + tq_io) // SEL_LEN == col_t // SEL_LEN) & (t0 + tq_io <= col_t)
    s = jnp.where(own, s_own, -MASK_BIG)
    carry = online(s, vst_ref[:, pl.ds(t0a, TQ)], init)

    def sel_scores(tile):
        rows = pl.ds(pl.multiple_of(tile * TK_SEL, TK_SEL), TK_SEL)
        return _dot(jnp.concatenate([ks_ref[rows, :], blk_ref[rows, :]], axis=1), qaug_scr[...])

    last_tile = ks_ref.shape[0] // TK_SEL - 1
    blocks_per_tile = TK_SEL // SEL_LEN
    for ahead in range(SEL_AHEAD):
        per_block = [jnp.broadcast_to(bias_f32[j:j + 1, :], (SEL_LEN, TQ))
                     for j in range(ahead * blocks_per_tile, (ahead + 1) * blocks_per_tile)]
        s_scr[ahead] = s_first[ahead] + jnp.tile(jnp.concatenate(per_block, axis=0), (1, HPG))

    def sel_group(i, carry):
        first = SEL_GROUP * i
        scores = [s_scr[ahead] for ahead in range(SEL_AHEAD)]
        issue = lambda: scores.append(sel_scores(jnp.minimum(first + len(scores), last_tile)))
        for _ in range(SEL_BURST):
            issue()
        for part in range(SEL_GROUP):
            k0 = pl.multiple_of((first + part) * TK_SEL, TK_SEL)
            carry = online(scores[part], vst_ref[:, pl.ds(k0, TK_SEL)], carry)
            if len(scores) < SEL_GROUP + SEL_AHEAD:
                issue()
        for ahead in range(SEL_AHEAD):
            s_scr[ahead] = scores[SEL_GROUP + ahead]
        return carry

    acc_s, l_s = finish(lax.fori_loop(0, t0 // (SEL_GROUP * TK_SEL) + 1, sel_group, carry))

    gate = lambda c: jnp.concatenate([g_ref[c, h:h + 1, :] for h in range(HPG)], axis=1)
    out = gate(0) * o_c + (gate(1) * (1.0 / l_s)) * acc_s + (gate(2) * (1.0 / l_w)) * acc_w
    for h in range(HPG):
        o_ref[h] = out[:, h * TQ:(h + 1) * TQ].astype(o_ref.dtype)


def _nsa(qt, ks, blk, vst, kw, vwt, kc, vct, c2s, gates):
    bsz, ng, _, _, seq = qt.shape
    n_cmp = kc.shape[2]
    n_blk = c2s.shape[0]
    per_group = lambda *shape: pl.BlockSpec((None, None) + shape, lambda bi, gi, qi: (bi, gi) + (0,) * len(shape))
    q_tile = lambda *lead: pl.BlockSpec((None, None) + lead + (TQ,), lambda bi, gi, qi: (bi, gi) + (0,) * len(lead) + (qi,))
    key_slot = pl.BlockSpec((None, seq, LANES), lambda bi, gi, qi: (bi, 0, gi))
    return pl.pallas_call(
        _nsa_kernel,
        grid=(bsz, ng, seq // TQ),
        in_specs=[
            q_tile(HPG, HEAD_DIM),
            key_slot, pl.BlockSpec((seq, LANES), lambda bi, gi, qi: (0, 0)), per_group(V_ROWS, seq),
            key_slot, per_group(V_ROWS, seq),
            per_group(n_cmp, HEAD_DIM), per_group(HEAD_DIM, n_cmp),
            pl.BlockSpec((n_blk, n_cmp), lambda bi, gi, qi: (0, 0)),
            q_tile(3, HPG),
        ],
        out_specs=q_tile(HPG, HEAD_DIM),
        out_shape=jax.ShapeDtypeStruct((bsz, ng, HPG, HEAD_DIM, seq), qt.dtype),
        scratch_shapes=[pltpu.VMEM((2 * LANES, HPG * TQ), qt.dtype), pltpu.VMEM((SEL_AHEAD, TK_SEL, HPG * TQ), F32)],
        compiler_params=_params("parallel", "parallel", "arbitrary"),
        name="nsa",
    )(qt, ks, blk, vst, kw, vwt, kc, vct, c2s, gates)


def _merge_kernel(x_ref, hc_ref, aot_ref, gc_ref, gn_ref, wc_ref, wn_ref, wo_ref, o_ref):
    y_conv = _dot(hc_ref[...], wc_ref[...])
    y_nsa = lax.dot_general(aot_ref[...], wn_ref[...], (((0,), (0,)), ((), ())), preferred_element_type=F32)
    mixed = gc_ref[...] * y_conv + gn_ref[...] * y_nsa
    o_ref[...] = x_ref[...] + _dot(mixed.astype(wo_ref.dtype), wo_ref[...])


def _merge(x, hc, aot, gates, wc, wn, wo):
    n = x.shape[0]
    seq = aot.shape[2]
    t_tiles = seq // TM_OUT
    row = lambda j: pl.BlockSpec((TM_OUT, D_MODEL), lambda i: (i, j))
    w_spec = pl.BlockSpec((D_MODEL, D_MODEL), lambda i: (0, 0))
    aot_spec = pl.BlockSpec((None, D_MODEL, TM_OUT), lambda i: (i // t_tiles, 0, i % t_tiles))
    return pl.pallas_call(
        _merge_kernel,
        grid=(n // TM_OUT,),
        in_specs=[row(0), row(0), aot_spec, row(0), row(1), w_spec, w_spec, w_spec],
        out_specs=row(0),
        out_shape=jax.ShapeDtypeStruct((n, D_MODEL), F32),
        compiler_params=_params("parallel"),
        name="merge_out",
    )(x, hc, aot, gates, gates, wc, wn, wo)


def _ple_kernel(x_ref, p_ref, g_ref, gp_ref, wg_ref, wp_ref, o_ref):
    x = x_ref[...]
    e = _rms(_dot(p_ref[...].astype(wp_ref.dtype), wp_ref[...]), gp_ref[...])
    gate = jax.nn.sigmoid(_dot(_rms(x, g_ref[...]).astype(wg_ref.dtype), wg_ref[...]))
    o_ref[...] = x + gate * e


def _ple(x, p, g, g_post, w_gate, w_proj):
    n = x.shape[0]
    vec = pl.BlockSpec((1, D_MODEL), lambda i: (0, 0))
    return pl.pallas_call(
        _ple_kernel,
        grid=(n // TM_OUT,),
        in_specs=[
            pl.BlockSpec((TM_OUT, D_MODEL), lambda i: (i, 0)),
            pl.BlockSpec((TM_OUT, PLE_DIM), lambda i: (i, 0)),
            vec, vec,
            pl.BlockSpec((D_MODEL, D_MODEL), lambda i: (0, 0)),
            pl.BlockSpec((PLE_DIM, D_MODEL), lambda i: (0, 0)),
        ],
        out_specs=pl.BlockSpec((TM_OUT, D_MODEL), lambda i: (i, 0)),
        out_shape=jax.ShapeDtypeStruct((n, D_MODEL), F32),
        compiler_params=_params("parallel"),
        name="ple",
    )(x, p, g, g_post, w_gate, w_proj)


def _rope_tables(pos, width):
    half = ROPE_DIM // 2
    inv = ROPE_THETA ** (-jnp.arange(half, dtype=F32) / half)
    ang = pos.astype(F32)[:, None] * inv[None, :]
    cos, sin = jnp.cos(ang), jnp.sin(ang)
    n = pos.shape[0]
    rest = HEAD_DIM - ROPE_DIM
    seg_cos = jnp.concatenate([cos, cos, jnp.ones((n, rest), F32)], axis=1)
    seg_lo = jnp.concatenate([-sin, jnp.zeros((n, HEAD_DIM - half), F32)], axis=1)
    seg_hi = jnp.concatenate([jnp.zeros((n, half), F32), sin, jnp.zeros((n, rest), F32)], axis=1)
    reps = width // HEAD_DIM
    return tuple(jnp.tile(t, (1, reps)) for t in (seg_cos, seg_lo, seg_hi))


def kernel(x, p, ffn1_norm, ffn1_w_in, ffn1_w_out, mix_norm, w_in, b_in, conv_w, conv_b, conv_ln_g, conv_ln_b, conv_w_out, q_norm, k_norm, cmp_pos, cmp_w1, cmp_b1, cmp_w2, nsa_w_out, w_out, ffn2_norm, ffn2_w_in, ffn2_w_out, ple_norm, ple_w_gate, ple_w_proj, ple_post_norm):
    bsz, seq, _ = x.shape
    depth = w_in.shape[0]
    n_rows = bsz * seq
    n_sub = seq // CMP_STRIDE
    n_blk = seq // SEL_LEN
    mx = MXU_DTYPE
    row = lambda v: v[:, None, :]

    def kv_cols(j):
        return slice(OFF_KV + j * KV_W, OFF_KV + (j + 1) * KV_W)

    def group_slots(a, width, fill=None):
        a = a.reshape(a.shape[:-1] + (N_KV, HEAD_DIM))
        tail = jnp.zeros(a.shape[:-1] + (width - HEAD_DIM,), a.dtype)
        if fill is not None:
            tail = tail.at[..., 0].set(fill)
        a = jnp.concatenate([a, tail], axis=-1)
        return a.reshape(a.shape[:-2] + (N_KV * width,))

    def gate_order(a):
        a = a.reshape(a.shape[:-1] + (N_KV, HPG, 3))
        return jnp.swapaxes(a, -1, -2).reshape(a.shape[:-3] + (3 * N_HEADS,))

    def split(w, transposed_fill):
        rows_part = jnp.concatenate(
            [w[..., 0:OFF_Q], w[..., OFF_MERGE:N_IN], group_slots(w[..., kv_cols(2)], LANES),
             group_slots(w[..., kv_cols(4)], LANES), w[..., kv_cols(0)], w[..., kv_cols(1)]], axis=-1)
        gates_part = gate_order(w[..., OFF_GATE:OFF_MERGE])
        pad = jnp.zeros(w.shape[:-1] + (TR_END - TR_NG - 3 * N_HEADS,), w.dtype)
        t_part = jnp.concatenate(
            [w[..., OFF_Q:OFF_KV], group_slots(w[..., kv_cols(3)], V_ROWS, transposed_fill),
             group_slots(w[..., kv_cols(5)], V_ROWS, transposed_fill), gates_part, pad], axis=-1)
        return rows_part, t_part

    w_rows, w_t = split(w_in, None)
    w_rows, w_t = w_rows.astype(mx), jnp.swapaxes(w_t, 1, 2).astype(mx)
    b_rows, b_t = split(b_in, 1.0)
    lane_rep = lambda v: jnp.broadcast_to(v[:, :, None], v.shape + (LANES,))
    b_rows, b_t = row(b_rows), lane_rep(b_t)

    q_gain_t = lane_rep(jnp.tile(q_norm, (1, N_HEADS)) * (HEAD_DIM ** -0.5 * LOG2_E))
    k_gain = row(jnp.concatenate([group_slots(jnp.tile(k_norm[:, 1], (1, N_KV)), LANES),
                                  group_slots(jnp.tile(k_norm[:, 2], (1, N_KV)), LANES)], axis=-1))
    cmp_gain = jnp.concatenate([k_norm[:, 0], jnp.zeros((depth, LANES - HEAD_DIM), F32)], axis=-1)[:, None, :]
    w2_pad = jnp.concatenate([cmp_w2, jnp.zeros(cmp_w2.shape[:-1] + (LANES - HEAD_DIM,), F32)], axis=-1).astype(mx)
    sub_w = CMP_STRIDE * HEAD_DIM
    pos_flat = cmp_pos.reshape(depth, 2, 2, 1, sub_w)

    layers = dict(
        ffn1_norm=row(ffn1_norm), ffn1_w_in=ffn1_w_in.astype(mx), ffn1_w_out=ffn1_w_out.astype(mx),
        mix_norm=row(mix_norm), w_rows=w_rows, b_rows=b_rows, w_t=w_t, b_t=b_t, k_gain=k_gain, q_gain_t=q_gain_t,
        conv_w=jnp.broadcast_to(conv_w[:, :, None, :], conv_w.shape[:2] + (SUBLANES, CONV_DIM)), conv_b=row(conv_b), conv_ln_g=row(conv_ln_g), conv_ln_b=row(conv_ln_b),
        conv_w_out=conv_w_out.astype(mx), cmp_gain=cmp_gain, pos_flat=pos_flat,
        cmp_w1=cmp_w1.astype(mx), cmp_b1=cmp_b1[:, :, None, :], cmp_w2=w2_pad,
        nsa_w_out=nsa_w_out.astype(mx), w_out=w_out.astype(mx),
        ffn2_norm=row(ffn2_norm), ffn2_w_in=ffn2_w_in.astype(mx), ffn2_w_out=ffn2_w_out.astype(mx),
        ple_norm=row(ple_norm), ple_w_gate=ple_w_gate.astype(mx), ple_w_proj=ple_w_proj.astype(mx),
        ple_post_norm=row(ple_post_norm), p=p.reshape(depth, n_rows, PLE_DIM),
    )

    tok_cos, tok_lo, tok_hi = _rope_tables(jnp.arange(seq), TN)
    cmp_cos, cmp_lo, cmp_hi = _rope_tables(jnp.arange(n_sub) * CMP_STRIDE + CMP_LEN - 1, LANES)
    seg = jnp.arange(TN) // HEAD_DIM
    ones_bd = (seg[:, None] == seg[None, :]).astype(mx)
    ci = jnp.arange(n_sub)[None, :]
    sj = jnp.arange(n_blk)[:, None]
    c2s = ((ci * CMP_STRIDE < (sj + 1) * SEL_LEN) & (ci * CMP_STRIDE + CMP_LEN > sj * SEL_LEN)).astype(mx)
    blk_of_key = ((jnp.arange(seq) // SEL_LEN)[:, None] == jnp.arange(LANES)[None, :]).astype(mx)
    half = ROPE_DIM // 2
    ang_t = (ROPE_THETA ** (-jnp.arange(half, dtype=F32) / half))[:, None] * jnp.arange(seq).astype(F32)[None, :]
    cos_t, sin_t = jnp.cos(ang_t), jnp.sin(ang_t)

    def sub_blocks(a):
        a = a.reshape(bsz, n_sub, CMP_STRIDE, N_KV, HEAD_DIM).transpose(0, 3, 1, 2, 4)
        return a.reshape(bsz, N_KV, n_sub, sub_w)

    def layer(xf, lp):
        xf = _ffn(xf, lp["ffn1_norm"], lp["ffn1_w_in"], lp["ffn1_w_out"])

        h_glu, merge_gates, ks, kw, plain, qt, vst, vwt, gates = _mix_proj(
            xf, lp["mix_norm"], lp["w_rows"], lp["b_rows"], lp["w_t"], lp["b_t"], lp["k_gain"], lp["q_gain_t"],
            ones_bd, tok_cos, tok_lo, tok_hi, cos_t, sin_t, bsz, seq)

        hc = _conv_module(h_glu.reshape(bsz, seq, CONV_DIM), lp["conv_w"], lp["conv_b"],
                          lp["conv_ln_g"], lp["conv_ln_b"], mx).reshape(n_rows, CONV_DIM)

        comp = []
        for j, is_key in ((0, True), (1, False)):
            pos = lp["pos_flat"][j]
            comp.append(_compress(
                sub_blocks(plain[:, j * KV_W:(j + 1) * KV_W]), pos[0], pos[1],
                lp["cmp_w1"][j, :sub_w], lp["cmp_w1"][j, sub_w:], lp["cmp_b1"][j], lp["cmp_w2"][j],
                lp["cmp_gain"], cmp_cos, cmp_lo, cmp_hi, is_key=is_key, out_dtype=mx))
        kc = comp[0]
        vct = comp[1].transpose(0, 1, 3, 2)

        aot = _nsa(qt.reshape(bsz, N_KV, HPG, HEAD_DIM, seq), ks.reshape(bsz, seq, K_PAD_W), blk_of_key,
                   vst.reshape(bsz, N_KV, V_ROWS, seq), kw.reshape(bsz, seq, K_PAD_W),
                   vwt.reshape(bsz, N_KV, V_ROWS, seq), kc, vct, c2s,
                   gates.reshape(bsz, N_KV, 3, HPG, seq))
        aot = aot.reshape(bsz, N_HEADS * HEAD_DIM, seq)

        xf = _merge(xf, hc, aot, merge_gates, lp["conv_w_out"], lp["nsa_w_out"], lp["w_out"])
        xf = _ffn(xf, lp["ffn2_norm"], lp["ffn2_w_in"], lp["ffn2_w_out"])
        xf = _ple(xf, lp["p"], lp["ple_norm"], lp["ple_post_norm"], lp["ple_w_gate"], lp["ple_w_proj"])
        return xf, None

    out, _ = lax.scan(layer, x.reshape(n_rows, D_MODEL), layers)
    return out.reshape(bsz, seq, D_MODEL)
```

```python
import functools

import jax
import jax.numpy as jnp
from jax import lax
from jax.experimental import pallas as pl
from jax.experimental.pallas import tpu as pltpu

D_MODEL = 1024
PLE_DIM = 256
D_FF = 2816
CONV_DIM = D_MODEL
CONV_K = 31
N_HEADS = 16
N_KV = 4
HPG = N_HEADS // N_KV
HEAD_DIM = 64
ROPE_DIM = HEAD_DIM // 4
ROPE_THETA = 500000.0
CMP_LEN = 32
CMP_STRIDE = 16
CMP_HID = 256
SEL_LEN = 64
N_SEL = 16
WINDOW = 512
NEG = -1e30
EPS = 1e-6
KV_W = N_KV * HEAD_DIM
OFF_Q = 2 * CONV_DIM
OFF_KV = OFF_Q + N_HEADS * HEAD_DIM
OFF_GATE = OFF_KV + 6 * KV_W
OFF_MERGE = OFF_GATE + 3 * N_HEADS
N_IN = OFF_MERGE + 2 * D_MODEL

MXU_DTYPE = jnp.bfloat16
F32 = jnp.float32
LANES = 128
SUBLANES = 8
VMEM_LIMIT = 56 * 1024 * 1024

TM_FFN = 512
TN = 256
TF = 256
TM_OUT = 512
TT = 256
HALO = 32
CONV_ROWS = 32
CONV_COLS = 512
TQ = 128
TK_SEL = 256
SEL_GROUP = 4
SEL_AHEAD = 4
SEL_BURST = 2
MASK_BIG = 2.0 ** 100
LOG2_E = 1.4426950408889634
V_ROWS = HEAD_DIM + 16

TM_PROJ = 512
K_PAD_W = N_KV * LANES
RM_GLU_A, RM_GLU_G, RM_MERGE = 0, CONV_DIM, 2 * CONV_DIM
RM_KS = RM_MERGE + 2 * D_MODEL
RM_KW = RM_KS + K_PAD_W
RM_KVC = RM_KW + K_PAD_W
RM_END = RM_KVC + 2 * KV_W
TR_VS = N_HEADS * HEAD_DIM
TR_VW = TR_VS + N_KV * V_ROWS
TR_NG = TR_VW + N_KV * V_ROWS
TR_END = TR_NG + 64


def _params(*sem):
    return pltpu.CompilerParams(dimension_semantics=sem, vmem_limit_bytes=VMEM_LIMIT)


def _rms(x, g):
    return x * lax.rsqrt(jnp.mean(x * x, axis=-1, keepdims=True) + EPS) * g


def _dot(a, b):
    return jnp.dot(a, b, preferred_element_type=F32)


def _col_reduce(reduce_fn, combine_fn, s):
    rows = [s[r:r + SUBLANES] for r in range(0, s.shape[0], SUBLANES)]
    while len(rows) > 1:
        paired = [combine_fn(rows[i], rows[i + 1]) for i in range(0, len(rows) - 1, 2)]
        rows = paired + rows[len(rows) - len(rows) % 2:]
    return reduce_fn(rows[0], axis=0, keepdims=True)


def _ffn_kernel(x_ref, g_ref, wi_ref, wo_ref, *rest):
    o_ref = rest[-1]
    x = x_ref[...]
    h = _rms(x, g_ref[...]).astype(wi_ref.dtype)
    acc = None
    for f0 in range(0, D_FF, TF):
        a = _dot(h, wi_ref[:, f0:f0 + TF])
        b = _dot(h, wi_ref[:, D_FF + f0:D_FF + f0 + TF])
        act = (a * jax.nn.sigmoid(a) * b).astype(wo_ref.dtype)
        part = _dot(act, wo_ref[f0:f0 + TF, :])
        acc = part if acc is None else acc + part
    y = x + 0.5 * acc
    if len(rest) > 1:
        p_ref, pg_ref, gp_ref, wg_ref, wp_ref = rest[:-1]
        e = _rms(_dot(p_ref[...].astype(wp_ref.dtype), wp_ref[...]), gp_ref[...])
        gate = jax.nn.sigmoid(_dot(_rms(y, pg_ref[...]).astype(wg_ref.dtype), wg_ref[...]))
        y = y + gate * e
    o_ref[...] = y


def _ffn(x, g, w_in, w_out, ple=None):
    n = x.shape[0]
    const = lambda shape: pl.BlockSpec(shape, lambda i: (0,) * len(shape), pipeline_mode=pl.Buffered(1))
    by_row = lambda width: pl.BlockSpec((TM_FFN, width), lambda i: (i, 0))
    args = [x, g, w_in, w_out]
    in_specs = [by_row(D_MODEL), const((1, D_MODEL)), const(w_in.shape), const(w_out.shape)]
    if ple is not None:
        args += list(ple)
        in_specs += [by_row(PLE_DIM)] + [const(a.shape) for a in ple[1:]]
    return pl.pallas_call(
        _ffn_kernel,
        grid=(n // TM_FFN,),
        in_specs=in_specs,
        out_specs=by_row(D_MODEL),
        out_shape=jax.ShapeDtypeStruct((n, D_MODEL), F32),
        compiler_params=_params("parallel"),
        name="ffn" if ple is None else "ffn_ple",
    )(*args)


def _head_norm_rope(y, ones_bd, gain, cos, sin_lo, sin_hi):
    ss = _dot((y * y).astype(ones_bd.dtype), ones_bd)
    yn = y * lax.rsqrt(ss * (1.0 / HEAD_DIM) + EPS) * gain
    width = y.shape[-1]
    half = ROPE_DIM // 2
    return (yn * cos + pltpu.roll(yn, width - half, 1) * sin_lo + pltpu.roll(yn, half, 1) * sin_hi)


def _mix_proj_kernel(x_ref, g_ref, wr_ref, br_ref, wt_ref, bt_ref, kg_ref, qg_ref, bd_ref,
                     cos_ref, slo_ref, shi_ref, cost_ref, sint_ref,
                     hglu_ref, mg_ref, ks_ref, kw_ref, kvc_ref, qt_ref, vst_ref, vwt_ref, ngt_ref):
    h = _rms(x_ref[...], g_ref[...]).astype(wr_ref.dtype)
    tm = h.shape[0]

    def rows(lo, hi):
        return _dot(h, wr_ref[:, lo:hi]) + br_ref[:, lo:hi]

    hglu_ref[...] = rows(RM_GLU_A, RM_GLU_G) * jax.nn.sigmoid(rows(RM_GLU_G, RM_MERGE))
    mg_ref[...] = jax.nn.sigmoid(rows(RM_MERGE, RM_KS))
    for out_ref, base in ((ks_ref, RM_KS), (kw_ref, RM_KW)):
        for c0 in range(0, K_PAD_W, TN):
            y = rows(base + c0, base + c0 + TN)
            gain = kg_ref[:, base - RM_KS + c0:base - RM_KS + c0 + TN]
            out = _head_norm_rope(y, bd_ref[...], gain, cos_ref[...], slo_ref[...], shi_ref[...])
            out_ref[:, c0:c0 + TN] = out.astype(out_ref.dtype)
    kvc_ref[...] = rows(RM_KVC, RM_END)

    lanes = lambda ref, lo, hi: jnp.tile(ref[lo:hi, :], (1, tm // LANES))
    yt = lax.dot_general(wt_ref[...], h, (((1,), (1,)), ((), ())), preferred_element_type=F32)

    half = ROPE_DIM // 2
    cos_t, sin_t = cost_ref[...], sint_ref[...]
    for hd in range(N_HEADS):
        r0 = hd * HEAD_DIM
        y = yt[r0:r0 + HEAD_DIM] + lanes(bt_ref, r0, r0 + HEAD_DIM)
        ss = jnp.sum(y * y, axis=0, keepdims=True)
        yn = y * lax.rsqrt(ss * (1.0 / HEAD_DIM) + EPS) * lanes(qg_ref, r0, r0 + HEAD_DIM)
        x1, x2 = yn[0:half], yn[half:ROPE_DIM]
        rot = jnp.concatenate([x1 * cos_t - x2 * sin_t, x2 * cos_t + x1 * sin_t, yn[ROPE_DIM:]], axis=0)
        qt_ref[r0:r0 + HEAD_DIM, :] = rot.astype(qt_ref.dtype)

    vst_ref[...] = (yt[TR_VS:TR_VW] + lanes(bt_ref, TR_VS, TR_VW)).astype(vst_ref.dtype)
    vwt_ref[...] = (yt[TR_VW:TR_NG] + lanes(bt_ref, TR_VW, TR_NG)).astype(vwt_ref.dtype)
    n_gate = ngt_ref.shape[0]
    ngt_ref[...] = jax.nn.sigmoid(yt[TR_NG:TR_NG + n_gate] + lanes(bt_ref, TR_NG, TR_NG + n_gate))


def _mix_proj(x, g, w_rows, b_rows, w_t, b_t, k_gain, q_gain_t, ones_bd, cos, sin_lo, sin_hi, cos_t, sin_t, bsz, seq):
    n = x.shape[0]
    t_tiles = seq // TM_PROJ
    const = lambda shape: pl.BlockSpec(shape, lambda i: (0,) * len(shape), pipeline_mode=pl.Buffered(1))
    by_row = lambda width: pl.BlockSpec((TM_PROJ, width), lambda i: (i, 0))
    by_time = lambda n_rows: pl.BlockSpec((None, n_rows, TM_PROJ), lambda i: (i // t_tiles, 0, i % t_tiles))
    mx = w_rows.dtype
    n_gate = 3 * N_HEADS
    return pl.pallas_call(
        _mix_proj_kernel,
        grid=(n // TM_PROJ,),
        in_specs=[
            by_row(D_MODEL), const((1, D_MODEL)),
            const(w_rows.shape), const(b_rows.shape), const(w_t.shape), const(b_t.shape),
            const(k_gain.shape), const(q_gain_t.shape), const(ones_bd.shape),
            pl.BlockSpec((TM_PROJ, TN), lambda i: (i % t_tiles, 0)),
            pl.BlockSpec((TM_PROJ, TN), lambda i: (i % t_tiles, 0)),
            pl.BlockSpec((TM_PROJ, TN), lambda i: (i % t_tiles, 0)),
            pl.BlockSpec((ROPE_DIM // 2, TM_PROJ), lambda i: (0, i % t_tiles)),
            pl.BlockSpec((ROPE_DIM // 2, TM_PROJ), lambda i: (0, i % t_tiles)),
        ],
        out_specs=[
            by_row(CONV_DIM), by_row(2 * D_MODEL), by_row(K_PAD_W), by_row(K_PAD_W), by_row(2 * KV_W),
            by_time(N_HEADS * HEAD_DIM), by_time(N_KV * V_ROWS), by_time(N_KV * V_ROWS), by_time(n_gate),
        ],
        out_shape=[
            jax.ShapeDtypeStruct((n, CONV_DIM), F32), jax.ShapeDtypeStruct((n, 2 * D_MODEL), F32),
            jax.ShapeDtypeStruct((n, K_PAD_W), mx), jax.ShapeDtypeStruct((n, K_PAD_W), mx),
            jax.ShapeDtypeStruct((n, 2 * KV_W), F32),
            jax.ShapeDtypeStruct((bsz, N_HEADS * HEAD_DIM, seq), mx),
            jax.ShapeDtypeStruct((bsz, N_KV * V_ROWS, seq), mx), jax.ShapeDtypeStruct((bsz, N_KV * V_ROWS, seq), mx),
            jax.ShapeDtypeStruct((bsz, n_gate, seq), F32),
        ],
        compiler_params=_params("parallel"),
        name="mix_proj",
    )(x, g, w_rows, b_rows, w_t, b_t, k_gain, q_gain_t, ones_bd, cos, sin_lo, sin_hi, cos_t, sin_t)


def _conv_kernel(cur_ref, halo_ref, w_ref, b_ref, lg_ref, lb_ref, o_ref, sh_scr, conv_scr):
    t = pl.program_id(1)
    ext = TT + HALO - 8
    sh_scr[0, 0:HALO, :] = jnp.where(t == 0, 0.0, halo_ref[...])
    sh_scr[0, HALO:HALO + TT, :] = cur_ref[...]
    for r in range(1, 8):
        sh_scr[r, 0:ext, :] = sh_scr[0, r:r + ext, :]

    lead = HALO - (CONV_K - 1)

    def row_chunk(ri, carry):
        r0 = pl.multiple_of(ri * CONV_ROWS, CONV_ROWS)
        for c0 in range(0, CONV_DIM, CONV_COLS):
            groups = CONV_ROWS // SUBLANES
            acc = jnp.broadcast_to(b_ref[:, c0:c0 + CONV_COLS], (groups, SUBLANES, CONV_COLS))
            for k in range(CONV_K):
                off = k + lead
                base = pl.multiple_of(r0 + (off // 8) * 8, 8)
                rows = sh_scr[off % 8, pl.ds(base, CONV_ROWS), c0:c0 + CONV_COLS]
                acc = acc + w_ref[k, :, c0:c0 + CONV_COLS] * rows.reshape(groups, SUBLANES, CONV_COLS)
            conv_scr[pl.ds(r0, CONV_ROWS), c0:c0 + CONV_COLS] = acc.reshape(CONV_ROWS, CONV_COLS)
        return carry

    lax.fori_loop(0, TT // CONV_ROWS, row_chunk, 0)

    h = conv_scr[...]
    mu = jnp.mean(h, axis=-1, keepdims=True)
    var = jnp.mean(jnp.square(h - mu), axis=-1, keepdims=True)
    y = (h - mu) * lax.rsqrt(var + EPS) * lg_ref[...] + lb_ref[...]
    o_ref[...] = (y * jax.nn.sigmoid(y)).astype(o_ref.dtype)


def _conv_module(h, w, b, ln_g, ln_b, out_dtype):
    bsz, seq, _ = h.shape
    halo_per_tile = TT // HALO
    vec = pl.BlockSpec((1, CONV_DIM), lambda bi, t: (0, 0))
    return pl.pallas_call(
        _conv_kernel,
        grid=(bsz, seq // TT),
        in_specs=[
            pl.BlockSpec((None, TT, CONV_DIM), lambda bi, t: (bi, t, 0)),
            pl.BlockSpec((None, HALO, CONV_DIM), lambda bi, t: (bi, jnp.maximum(t * halo_per_tile - 1, 0), 0)),
            pl.BlockSpec((CONV_K, SUBLANES, CONV_DIM), lambda bi, t: (0, 0, 0)),
            vec, vec, vec,
        ],
        out_specs=pl.BlockSpec((None, TT, CONV_DIM), lambda bi, t: (bi, t, 0)),
        out_shape=jax.ShapeDtypeStruct((bsz, seq, CONV_DIM), out_dtype),
        scratch_shapes=[pltpu.VMEM((8, TT + HALO, CONV_DIM), F32), pltpu.VMEM((TT, CONV_DIM), F32)],
        compiler_params=_params("parallel", "parallel"),
        name="conv_module",
    )(h, h, w, b, ln_g, ln_b)


def _compress_kernel(a_ref, plo_ref, phi_ref, w1a_ref, w1b_ref, b1_ref, w2_ref,
                     hg_ref, cos_ref, slo_ref, shi_ref, o_ref, *, is_key):
    a = a_ref[...]
    n = a.shape[0]
    p = _dot((a + plo_ref[...]).astype(w1a_ref.dtype), w1a_ref[...])
    q = _dot((a + phi_ref[...]).astype(w1b_ref.dtype), w1b_ref[...])
    hid = p + pltpu.roll(q, n - 1, 0) + b1_ref[...]
    hid = jax.nn.gelu(hid, approximate=True)
    c = _dot(hid.astype(w2_ref.dtype), w2_ref[...])
    if is_key:
        ss = jnp.sum(c * c, axis=-1, keepdims=True)
        cn = c * lax.rsqrt(ss * (1.0 / HEAD_DIM) + EPS) * hg_ref[...]
        half = ROPE_DIM // 2
        c = cn * cos_ref[...] + pltpu.roll(cn, LANES - half, 1) * slo_ref[...] + pltpu.roll(cn, half, 1) * shi_ref[...]
    o_ref[...] = c[:, :HEAD_DIM].astype(o_ref.dtype)


def _compress(a, pos_lo, pos_hi, w1a, w1b, b1, w2, head_gain, cos, sin_lo, sin_hi, *, is_key, out_dtype):
    bsz, ng, n_sub, width = a.shape
    full = lambda *shape: pl.BlockSpec(shape, lambda bi, gi: (0,) * len(shape))
    return pl.pallas_call(
        functools.partial(_compress_kernel, is_key=is_key),
        grid=(bsz, ng),
        in_specs=[
            pl.BlockSpec((None, None, n_sub, width), lambda bi, gi: (bi, gi, 0, 0)),
            full(1, width), full(1, width), full(width, CMP_HID), full(width, CMP_HID),
            full(1, CMP_HID), full(CMP_HID, LANES), full(1, LANES),
            full(n_sub, LANES), full(n_sub, LANES), full(n_sub, LANES),
        ],
        out_specs=pl.BlockSpec((None, None, n_sub, HEAD_DIM), lambda bi, gi: (bi, gi, 0, 0)),
        out_shape=jax.ShapeDtypeStruct((bsz, ng, n_sub, HEAD_DIM), out_dtype),
        compiler_params=_params("parallel", "parallel"),
        name="compress_k" if is_key else "compress_v",
    )(a, pos_lo, pos_hi, w1a, w1b, b1, w2, head_gain, cos, sin_lo, sin_hi)


def _nsa_kernel(qt_ref, ks_ref, blk_ref, vst_ref, kw_ref, vwt_ref, kc_ref, vct_ref, c2s_ref, g_ref, o_ref,
                qaug_scr, s_scr):
    m_cols = HPG * TQ
    t0 = pl.program_id(2) * TQ
    qt = jnp.concatenate([qt_ref[h] for h in range(HPG)], axis=1)
    qt_pad = jnp.concatenate([qt, jnp.zeros((LANES - HEAD_DIM, m_cols), qt.dtype)], axis=0)
    col_t = t0 + (lax.broadcasted_iota(jnp.int32, (1, m_cols), 1) & (TQ - 1))

    n_cmp = kc_ref.shape[0]
    tq_io = lax.broadcasted_iota(jnp.int32, (TQ, 1), 0)
    t0a = pl.multiple_of(t0, TQ)
    w0 = pl.multiple_of(jnp.maximum(t0 - WINDOW, 0), TQ)
    s_cmp = _dot(kc_ref[...], qt)
    s_win = _dot(kw_ref[pl.ds(w0, WINDOW + TQ), :], qt_pad)
    s_own = _dot(ks_ref[pl.ds(t0a, TQ), :], qt_pad)
    s_first = [_dot(ks_ref[ahead * TK_SEL:(ahead + 1) * TK_SEL, :], qt_pad) for ahead in range(SEL_AHEAD)]

    s = s_cmp
    cmp_end = lax.broadcasted_iota(jnp.int32, (n_cmp, 1), 0) * CMP_STRIDE + (CMP_LEN - 1)
    cmask = cmp_end <= col_t
    s = jnp.where(cmask, s, NEG)
    p = jnp.exp2(s - jnp.maximum(_col_reduce(jnp.max, jnp.maximum, s), 0.1 * NEG))
    l = _col_reduce(jnp.sum, jnp.add, p)
    pc = p * (1.0 / jnp.where(l > 0.0, l, 1.0))
    o_c = _dot(vct_ref[...], pc.astype(vct_ref.dtype))

    pc_sum = pc[:, 0:TQ] + pc[:, TQ:2 * TQ] + pc[:, 2 * TQ:3 * TQ] + pc[:, 3 * TQ:4 * TQ]
    c2s = c2s_ref[...]
    hi = pc_sum.astype(c2s.dtype)
    lo = (pc_sum - hi.astype(F32)).astype(c2s.dtype)
    imp = _dot(c2s, hi) + _dot(c2s, lo)
    n_blk = imp.shape[0]
    j_io = lax.broadcasted_iota(jnp.int32, (n_blk, TQ), 0)
    t_io = t0 + lax.broadcasted_iota(jnp.int32, (n_blk, TQ), 1)
    cur = t_io // SEL_LEN
    forced = (j_io == 0) | (j_io == cur) | (j_io == cur - 1)
    valid = j_io * SEL_LEN <= t_io
    imp = jnp.where(valid, jnp.where(forced, 1e4, imp), -1.0)

    sel = jnp.where(forced & valid, 1.0, 0.0)
    imp = jnp.where(forced, -3e38, imp)
    j_f = j_io.astype(F32)
    for _ in range(max(min(N_SEL, n_blk) - 3, 0)):
        mx = jnp.max(imp, axis=0, keepdims=True)
        first = jnp.min(jnp.where(imp == mx, j_f, float(n_blk)), axis=0, keepdims=True)
        pick = j_f == first
        sel = jnp.where(pick & (mx >= 0.0), 1.0, sel)
        imp = jnp.where(pick, -3e38, imp)
    bias_f32 = (jnp.where(j_io == cur, 0.0, sel) - 1.0) * MASK_BIG
    bias = bias_f32.astype(qaug_scr.dtype)

    qaug_scr[0:LANES, :] = qt_pad
    if n_blk < LANES:
        qaug_scr[LANES + n_blk:, :] = jnp.zeros((LANES - n_blk, m_cols), qaug_scr.dtype)
    for h in range(HPG):
        qaug_scr[LANES:LANES + n_blk, h * TQ:(h + 1) * TQ] = bias

    s = s_win
    s = jnp.concatenate([jnp.where(w0 + tq_io > col_t - WINDOW, s[0:TQ], -MASK_BIG), s[TQ:]], axis=0)
    s = jnp.where(w0 + lax.broadcasted_iota(jnp.int32, (WINDOW + TQ, 1), 0) <= col_t, s, -MASK_BIG)
    p = jnp.exp2(s - _col_reduce(jnp.max, jnp.maximum, s))
    acc_w = _dot(vwt_ref[:, pl.ds(w0, WINDOW + TQ)], p.astype(vwt_ref.dtype))
    acc_w, l_w = acc_w[0:HEAD_DIM], acc_w[HEAD_DIM:HEAD_DIM + 1]

    def online(s, vt, carry):
        m_prev, acc = carry
        m_new = jnp.maximum(m_prev, _col_reduce(jnp.max, jnp.maximum, s))
        alpha = jnp.exp2(m_prev - m_new)
        p = jnp.exp2(s - m_new)
        return m_new, alpha * acc + _dot(vt, p.astype(vt.dtype))

    def finish(carry):
        _, acc = carry
        return acc[0:HEAD_DIM], acc[HEAD_DIM:HEAD_DIM + 1]

    init = (jnp.full((1, m_cols), -jnp.inf, F32), jnp.zeros((V_ROWS, m_cols), F32))

    own = ((t0 + tq_io) // SEL_LEN == col_t // SEL_LEN) & (t0 + tq_io <= col_t)
    s = jnp.where(own, s_own, -MASK_BIG)
    carry = online(s, vst_ref[:, pl.ds(t0a, TQ)], init)

    def sel_scores(tile):
        rows = pl.ds(pl.multiple_of(tile * TK_SEL, TK_SEL), TK_SEL)
        return _dot(jnp.concatenate([ks_ref[rows, :], blk_ref[rows, :]], axis=1), qaug_scr[...])

    last_tile = ks_ref.shape[0] // TK_SEL - 1
    blocks_per_tile = TK_SEL // SEL_LEN
    for ahead in range(SEL_AHEAD):
        per_block = [jnp.broadcast_to(bias_f32[j:j + 1, :], (SEL_LEN, TQ))
                     for j in range(ahead * blocks_per_tile, (ahead + 1) * blocks_per_tile)]
        s_scr[ahead] = s_first[ahead] + jnp.tile(jnp.concatenate(per_block, axis=0), (1, HPG))

    def sel_group(i, carry):
        first = SEL_GROUP * i
        scores = [s_scr[ahead] for ahead in range(SEL_AHEAD)]
        issue = lambda: scores.append(sel_scores(jnp.minimum(first + len(scores), last_tile)))
        for _ in range(SEL_BURST):
            issue()
        for part in range(SEL_GROUP):
            k0 = pl.multiple_of((first + part) * TK_SEL, TK_SEL)
            carry = online(scores[part], vst_ref[:, pl.ds(k0, TK_SEL)], carry)
            if len(scores) < SEL_GROUP + SEL_AHEAD:
                issue()
        for ahead in range(SEL_AHEAD):
            s_scr[ahead] = scores[SEL_GROUP + ahead]
        return carry

    acc_s, l_s = finish(lax.fori_loop(0, t0 // (SEL_GROUP * TK_SEL) + 1, sel_group, carry))

    gate = lambda c: jnp.concatenate([g_ref[c, h:h + 1, :] for h in range(HPG)], axis=1)
    out = gate(0) * o_c + (gate(1) * (1.0 / l_s)) * acc_s + (gate(2) * (1.0 / l_w)) * acc_w
    for h in range(HPG):
        o_ref[h] = out[:, h * TQ:(h + 1) * TQ].astype(o_ref.dtype)


def _nsa(qt, ks, blk, vst, kw, vwt, kc, vct, c2s, gates):
    bsz, ng, _, _, seq = qt.shape
    n_cmp = kc.shape[2]
    n_blk = c2s.shape[0]
    per_group = lambda *shape: pl.BlockSpec((None, None) + shape, lambda bi, gi, qi: (bi, gi) + (0,) * len(shape))
    q_tile = lambda *lead: pl.BlockSpec((None, None) + lead + (TQ,), lambda bi, gi, qi: (bi, gi) + (0,) * len(lead) + (qi,))
    key_slot = pl.BlockSpec((None, seq, LANES), lambda bi, gi, qi: (bi, 0, gi))
    return pl.pallas_call(
        _nsa_kernel,
        grid=(bsz, ng, seq // TQ),
        in_specs=[
            q_tile(HPG, HEAD_DIM),
            key_slot, pl.BlockSpec((seq, LANES), lambda bi, gi, qi: (0, 0)), per_group(V_ROWS, seq),
            key_slot, per_group(V_ROWS, seq),
            per_group(n_cmp, HEAD_DIM), per_group(HEAD_DIM, n_cmp),
            pl.BlockSpec((n_blk, n_cmp), lambda bi, gi, qi: (0, 0)),
            q_tile(3, HPG),
        ],
        out_specs=q_tile(HPG, HEAD_DIM),
        out_shape=jax.ShapeDtypeStruct((bsz, ng, HPG, HEAD_DIM, seq), qt.dtype),
        scratch_shapes=[pltpu.VMEM((2 * LANES, HPG * TQ), qt.dtype), pltpu.VMEM((SEL_AHEAD, TK_SEL, HPG * TQ), F32)],
        compiler_params=_params("parallel", "parallel", "arbitrary"),
        name="nsa",
    )(qt, ks, blk, vst, kw, vwt, kc, vct, c2s, gates)


def _merge_kernel(x_ref, hc_ref, aot_ref, gc_ref, gn_ref, wc_ref, wn_ref, wo_ref, o_ref):
    y_conv = _dot(hc_ref[...], wc_ref[...])
    y_nsa = lax.dot_general(aot_ref[...], wn_ref[...], (((0,), (0,)), ((), ())), preferred_element_type=F32)
    mixed = gc_ref[...] * y_conv + gn_ref[...] * y_nsa
    o_ref[...] = x_ref[...] + _dot(mixed.astype(wo_ref.dtype), wo_ref[...])


def _merge(x, hc, aot, gates, wc, wn, wo):
    n = x.shape[0]
    seq = aot.shape[2]
    t_tiles = seq // TM_OUT
    row = lambda j: pl.BlockSpec((TM_OUT, D_MODEL), lambda i: (i, j))
    w_spec = pl.BlockSpec((D_MODEL, D_MODEL), lambda i: (0, 0))
    aot_spec = pl.BlockSpec((None, D_MODEL, TM_OUT), lambda i: (i // t_tiles, 0, i % t_tiles))
    return pl.pallas_call(
        _merge_kernel,
        grid=(n // TM_OUT,),
        in_specs=[row(0), row(0), aot_spec, row(0), row(1), w_spec, w_spec, w_spec],
        out_specs=row(0),
        out_shape=jax.ShapeDtypeStruct((n, D_MODEL), F32),
        compiler_params=_params("parallel"),
        name="merge_out",
    )(x, hc, aot, gates, gates, wc, wn, wo)


def _rope_tables(pos, width):
    half = ROPE_DIM // 2
    inv = ROPE_THETA ** (-jnp.arange(half, dtype=F32) / half)
    ang = pos.astype(F32)[:, None] * inv[None, :]
    cos, sin = jnp.cos(ang), jnp.sin(ang)
    n = pos.shape[0]
    rest = HEAD_DIM - ROPE_DIM
    seg_cos = jnp.concatenate([cos, cos, jnp.ones((n, rest), F32)], axis=1)
    seg_lo = jnp.concatenate([-sin, jnp.zeros((n, HEAD_DIM - half), F32)], axis=1)
    seg_hi = jnp.concatenate([jnp.zeros((n, half), F32), sin, jnp.zeros((n, rest), F32)], axis=1)
    reps = width // HEAD_DIM
    return tuple(jnp.tile(t, (1, reps)) for t in (seg_cos, seg_lo, seg_hi))


def kernel(x, p, ffn1_norm, ffn1_w_in, ffn1_w_out, mix_norm, w_in, b_in, conv_w, conv_b, conv_ln_g, conv_ln_b, conv_w_out, q_norm, k_norm, cmp_pos, cmp_w1, cmp_b1, cmp_w2, nsa_w_out, w_out, ffn2_norm, ffn2_w_in, ffn2_w_out, ple_norm, ple_w_gate, ple_w_proj, ple_post_norm):
    bsz, seq, _ = x.shape
    depth = w_in.shape[0]
    n_rows = bsz * seq
    n_sub = seq // CMP_STRIDE
    n_blk = seq // SEL_LEN
    mx = MXU_DTYPE
    row = lambda v: v[:, None, :]

    def kv_cols(j):
        return slice(OFF_KV + j * KV_W, OFF_KV + (j + 1) * KV_W)

    def group_slots(a, width, fill=None):
        a = a.reshape(a.shape[:-1] + (N_KV, HEAD_DIM))
        tail = jnp.zeros(a.shape[:-1] + (width - HEAD_DIM,), a.dtype)
        if fill is not None:
            tail = tail.at[..., 0].set(fill)
        a = jnp.concatenate([a, tail], axis=-1)
        return a.reshape(a.shape[:-2] + (N_KV * width,))

    def gate_order(a):
        a = a.reshape(a.shape[:-1] + (N_KV, HPG, 3))
        return jnp.swapaxes(a, -1, -2).reshape(a.shape[:-3] + (3 * N_HEADS,))

    def split(w, transposed_fill):
        rows_part = jnp.concatenate(
            [w[..., 0:OFF_Q], w[..., OFF_MERGE:N_IN], group_slots(w[..., kv_cols(2)], LANES),
             group_slots(w[..., kv_cols(4)], LANES), w[..., kv_cols(0)], w[..., kv_cols(1)]], axis=-1)
        gates_part = gate_order(w[..., OFF_GATE:OFF_MERGE])
        pad = jnp.zeros(w.shape[:-1] + (TR_END - TR_NG - 3 * N_HEADS,), w.dtype)
        t_part = jnp.concatenate(
            [w[..., OFF_Q:OFF_KV], group_slots(w[..., kv_cols(3)], V_ROWS, transposed_fill),
             group_slots(w[..., kv_cols(5)], V_ROWS, transposed_fill), gates_part, pad], axis=-1)
        return rows_part, t_part

    w_rows, w_t = split(w_in, None)
    w_rows, w_t = w_rows.astype(mx), jnp.swapaxes(w_t, 1, 2).astype(mx)
    b_rows, b_t = split(b_in, 1.0)
    lane_rep = lambda v: jnp.broadcast_to(v[:, :, None], v.shape + (LANES,))
    b_rows, b_t = row(b_rows), lane_rep(b_t)

    q_gain_t = lane_rep(jnp.tile(q_norm, (1, N_HEADS)) * (HEAD_DIM ** -0.5 * LOG2_E))
    k_gain = row(jnp.concatenate([group_slots(jnp.tile(k_norm[:, 1], (1, N_KV)), LANES),
                                  group_slots(jnp.tile(k_norm[:, 2], (1, N_KV)), LANES)], axis=-1))
    cmp_gain = jnp.concatenate([k_norm[:, 0], jnp.zeros((depth, LANES - HEAD_DIM), F32)], axis=-1)[:, None, :]
    w2_pad = jnp.concatenate([cmp_w2, jnp.zeros(cmp_w2.shape[:-1] + (LANES - HEAD_DIM,), F32)], axis=-1).astype(mx)
    sub_w = CMP_STRIDE * HEAD_DIM
    pos_flat = cmp_pos.reshape(depth, 2, 2, 1, sub_w)

    layers = dict(
        ffn1_norm=row(ffn1_norm), ffn1_w_in=ffn1_w_in.astype(mx), ffn1_w_out=ffn1_w_out.astype(mx),
        mix_norm=row(mix_norm), w_rows=w_rows, b_rows=b_rows, w_t=w_t, b_t=b_t, k_gain=k_gain, q_gain_t=q_gain_t,
        conv_w=jnp.broadcast_to(conv_w[:, :, None, :], conv_w.shape[:2] + (SUBLANES, CONV_DIM)), conv_b=row(conv_b), conv_ln_g=row(conv_ln_g), conv_ln_b=row(conv_ln_b),
        conv_w_out=conv_w_out.astype(mx), cmp_gain=cmp_gain, pos_flat=pos_flat,
        cmp_w1=cmp_w1.astype(mx), cmp_b1=cmp_b1[:, :, None, :], cmp_w2=w2_pad,
        nsa_w_out=nsa_w_out.astype(mx), w_out=w_out.astype(mx),
        ffn2_norm=row(ffn2_norm), ffn2_w_in=ffn2_w_in.astype(mx), ffn2_w_out=ffn2_w_out.astype(mx),
        ple_norm=row(ple_norm), ple_w_gate=ple_w_gate.astype(mx), ple_w_proj=ple_w_proj.astype(mx),
        ple_post_norm=row(ple_post_norm), p=p.reshape(depth, n_rows, PLE_DIM),
    )

    tok_cos, tok_lo, tok_hi = _rope_tables(jnp.arange(seq), TN)
    cmp_cos, cmp_lo, cmp_hi = _rope_tables(jnp.arange(n_sub) * CMP_STRIDE + CMP_LEN - 1, LANES)
    seg = jnp.arange(TN) // HEAD_DIM
    ones_bd = (seg[:, None] == seg[None, :]).astype(mx)
    ci = jnp.arange(n_sub)[None, :]
    sj = jnp.arange(n_blk)[:, None]
    c2s = ((ci * CMP_STRIDE < (sj + 1) * SEL_LEN) & (ci * CMP_STRIDE + CMP_LEN > sj * SEL_LEN)).astype(mx)
    blk_of_key = ((jnp.arange(seq) // SEL_LEN)[:, None] == jnp.arange(LANES)[None, :]).astype(mx)
    half = ROPE_DIM // 2
    ang_t = (ROPE_THETA ** (-jnp.arange(half, dtype=F32) / half))[:, None] * jnp.arange(seq).astype(F32)[None, :]
    cos_t, sin_t = jnp.cos(ang_t), jnp.sin(ang_t)

    def sub_blocks(a):
        a = a.reshape(bsz, n_sub, CMP_STRIDE, N_KV, HEAD_DIM).transpose(0, 3, 1, 2, 4)
        return a.reshape(bsz, N_KV, n_sub, sub_w)

    def layer(xf, lp):
        xf = _ffn(xf, lp["ffn1_norm"], lp["ffn1_w_in"], lp["ffn1_w_out"])

        h_glu, merge_gates, ks, kw, plain, qt, vst, vwt, gates = _mix_proj(
            xf, lp["mix_norm"], lp["w_rows"], lp["b_rows"], lp["w_t"], lp["b_t"], lp["k_gain"], lp["q_gain_t"],
            ones_bd, tok_cos, tok_lo, tok_hi, cos_t, sin_t, bsz, seq)

        hc = _conv_module(h_glu.reshape(bsz, seq, CONV_DIM), lp["conv_w"], lp["conv_b"],
                          lp["conv_ln_g"], lp["conv_ln_b"], mx).reshape(n_rows, CONV_DIM)

        comp = []
        for j, is_key in ((0, True), (1, False)):
            pos = lp["pos_flat"][j]
            comp.append(_compress(
                sub_blocks(plain[:, j * KV_W:(j + 1) * KV_W]), pos[0], pos[1],
                lp["cmp_w1"][j, :sub_w], lp["cmp_w1"][j, sub_w:], lp["cmp_b1"][j], lp["cmp_w2"][j],
                lp["cmp_gain"], cmp_cos, cmp_lo, cmp_hi, is_key=is_key, out_dtype=mx))
        kc = comp[0]
        vct = comp[1].transpose(0, 1, 3, 2)

        aot = _nsa(qt.reshape(bsz, N_KV, HPG, HEAD_DIM, seq), ks.reshape(bsz, seq, K_PAD_W), blk_of_key,
                   vst.reshape(bsz, N_KV, V_ROWS, seq), kw.reshape(bsz, seq, K_PAD_W),
                   vwt.reshape(bsz, N_KV, V_ROWS, seq), kc, vct, c2s,
                   gates.reshape(bsz, N_KV, 3, HPG, seq))
        aot = aot.reshape(bsz, N_HEADS * HEAD_DIM, seq)

        xf = _merge(xf, hc, aot, merge_gates, lp["conv_w_out"], lp["nsa_w_out"], lp["w_out"])
        xf = _ffn(xf, lp["ffn2_norm"], lp["ffn2_w_in"], lp["ffn2_w_out"],
                  ple=(lp["p"], lp["ple_norm"], lp["ple_post_norm"], lp["ple_w_gate"], lp["ple_w_proj"]))
        return xf, None

    out, _ = lax.scan(layer, x.reshape(n_rows, D_MODEL), layers, unroll=True)
    return out.reshape(bsz, seq, D_MODEL)
```
